```python
import math, functools
import jax, jax.numpy as jnp
from jax import lax
import numpy as np

D_MODEL = 1024
BATCH = 4
SEQ = 8192
DEPTH = 1
DEC_BATCH = 128
DEC_SEQ = 4
PAST_LEN = 8192
PAGE_SIZE = 128

GLA_HEADS = 4
GLA_DK = D_MODEL // 2 // GLA_HEADS
GLA_DV = D_MODEL // GLA_HEADS
GLA_RANK = 16
GLA_TAU = 16.0
GLA_CHUNK = 64
DIFF_HEADS = 4
DIFF_DH = D_MODEL // (2 * DIFF_HEADS)
DIFF_DV = 2 * DIFF_DH
ROT_DIM = DIFF_DH // 4
ROPE_THETA = 500000.0
Q_BLOCK = 128
N_GROUPS = 4
EXPERTS_PER_GROUP = 4
N_EXPERTS = N_GROUPS * EXPERTS_PER_GROUP
D_EXPERT = D_MODEL // 2
TOP_K_IN_GROUP = 2
DN_ALPHA = (2.0 * DEPTH) ** 0.25
DN_BETA = (8.0 * DEPTH) ** -0.25
LN_EPS = 1e-5
RMS_EPS = 1e-5

IN_SIZES = (GLA_HEADS * GLA_DK, GLA_HEADS * GLA_DK, GLA_HEADS * GLA_DV, GLA_RANK, GLA_HEADS * GLA_DV,
            DIFF_HEADS * 2 * DIFF_DH, DIFF_HEADS * 2 * DIFF_DH, DIFF_HEADS * DIFF_DV, 2 * D_MODEL)
IN_BETA_SCALED = (False, False, True, False, False, False, False, True, False)
IN_OFFSETS = tuple(sum(IN_SIZES[:i + 1]) for i in range(len(IN_SIZES) - 1))
D_IN = sum(IN_SIZES)

kernel_name = "hybrid_gla_diffattn_hmoe_step"


def lambda_init(layer):
    return 0.8 - 0.6 * math.exp(-0.3 * layer)


def layer_norm(x, g, b):
    xf = x.astype(jnp.float32)
    mu = xf.mean(-1, keepdims=True)
    var = jnp.square(xf - mu).mean(-1, keepdims=True)
    return (xf - mu) * lax.rsqrt(var + LN_EPS) * g.astype(jnp.float32) + b.astype(jnp.float32)


def rms_norm(x, g):
    xf = x.astype(jnp.float32)
    return xf * lax.rsqrt(jnp.mean(jnp.square(xf), -1, keepdims=True) + RMS_EPS) * g.astype(jnp.float32)


def rotary(x, pos):
    half = ROT_DIM // 2
    inv_freq = ROPE_THETA ** (-jnp.arange(0, ROT_DIM, 2, dtype=jnp.float32) / ROT_DIM)
    ang = pos[:, None] * inv_freq[None, :]
    cos = jnp.cos(ang)[:, None, None, :]
    sin = jnp.sin(ang)[:, None, None, :]
    xf = x.astype(jnp.float32)
    x1, x2 = xf[..., :half], xf[..., half:ROT_DIM]
    out = jnp.concatenate([x1 * cos - x2 * sin, x2 * cos + x1 * sin, xf[..., ROT_DIM:]], axis=-1)
    return out.astype(x.dtype)


def gla_chunk(S, q, k, v, la):
    q, k, v, la = (t.astype(jnp.float32) for t in (q, k, v, la))
    C = q.shape[1]
    b = jnp.cumsum(la, axis=1)
    o_inter = jnp.einsum('bthk,bhkv->bthv', q * jnp.exp(b), S)
    causal = jnp.tril(jnp.ones((C, C), bool))[None, :, :, None, None]
    diff = b[:, :, None] - b[:, None, :]
    decay = jnp.where(causal, jnp.exp(jnp.where(causal, diff, 0.0)), 0.0)
    scores = jnp.einsum('bthk,bshk,btshk->bhts', q, k, decay)
    o_intra = jnp.einsum('bhts,bshv->bthv', scores, v)
    b_last = b[:, -1]
    k_dec = k * jnp.exp(b_last[:, None] - b)
    S_new = jnp.exp(b_last)[..., None] * S + jnp.einsum('bshk,bshv->bhkv', k_dec, v)
    return S_new, o_intra + o_inter


def gla_prompt(q, k, v, la):
    B, T = q.shape[:2]
    nc = T // GLA_CHUNK

    def to_chunks(t):
        return t.reshape(B, nc, GLA_CHUNK, *t.shape[2:]).swapaxes(0, 1)

    def body(S, xs):
        return gla_chunk(S, *xs)

    S0 = jnp.zeros((B, GLA_HEADS, GLA_DK, GLA_DV), jnp.float32)
    S_fin, o = lax.scan(body, S0, (to_chunks(q), to_chunks(k), to_chunks(v), to_chunks(la)))
    return S_fin, o.swapaxes(0, 1).reshape(B, T, GLA_HEADS, GLA_DV)


def gla_sample(S0, q, k, v, la):
    return gla_chunk(S0.astype(jnp.float32), q, k, v, la)


def diff_attn_core(q, k, v, valid, lam):
    s = jnp.einsum('...qhcd,...khcd->...hcqk', q, k).astype(jnp.float32) * (DIFF_DH ** -0.5)
    s = jnp.where(valid, s, -jnp.inf)
    p = jax.nn.softmax(s, axis=-1)
    a = p[..., 0, :, :] - lam * p[..., 1, :, :]
    return jnp.einsum('...hqk,...khv->...qhv', a, v.astype(jnp.float32))


def diff_attn_prompt(q, k, v, lam):
    B, T = q.shape[:2]
    key_pos = jnp.arange(T)

    def block(i):
        start = i * Q_BLOCK
        qb = lax.dynamic_slice_in_dim(q, start, Q_BLOCK, axis=1)
        valid = key_pos[None, :] <= (start + jnp.arange(Q_BLOCK))[:, None]
        return diff_attn_core(qb, k, v, valid, lam)

    o = lax.map(block, jnp.arange(T // Q_BLOCK))
    return o.swapaxes(0, 1).reshape(B, T, DIFF_HEADS, DIFF_DV)


def diff_attn_sample(q, k, v, lam, cache_k, cache_v, page_table, layer):
    T = q.shape[1]
    past_len = page_table.shape[1] * PAGE_SIZE
    key_pos = jnp.arange(past_len + T)
    valid = (key_pos[None, :] < past_len) | (key_pos[None, :] - past_len <= jnp.arange(T)[:, None])

    def one(args):
        qb, kb, vb, pages = args
        k_past = cache_k[layer, pages].reshape(past_len, DIFF_HEADS, 2, DIFF_DH)
        v_past = cache_v[layer, pages].reshape(past_len, DIFF_HEADS, DIFF_DV)
        kk = jnp.concatenate([k_past, kb.astype(k_past.dtype)], axis=0)
        vv = jnp.concatenate([v_past, vb.astype(v_past.dtype)], axis=0)
        return diff_attn_core(qb, kk, vv, valid, lam)

    return lax.map(one, (q, k, v, page_table))


def hier_moe(h, p):
    B, T, D = h.shape
    xt = h.reshape(-1, D)
    pg = jax.nn.softmax((xt @ p['w_group_router'] + p['b_group_router']).astype(jnp.float32), axis=-1)
    g_sel = jnp.argmax(pg, axis=-1)
    g_w = jnp.take_along_axis(pg, g_sel[:, None], axis=-1)
    el = (xt @ p['w_expert_router'] + p['b_expert_router']).astype(jnp.float32)
    el = el.reshape(-1, N_GROUPS, EXPERTS_PER_GROUP)
    el_g = jnp.take_along_axis(el, g_sel[:, None, None], axis=1)[:, 0]
    top_v, top_i = lax.top_k(el_g, TOP_K_IN_GROUP)
    w_sel = jax.nn.softmax(top_v, axis=-1) * g_w
    eid = g_sel[:, None] * EXPERTS_PER_GROUP + top_i
    combine = jnp.sum(jax.nn.one_hot(eid, N_EXPERTS, dtype=jnp.float32) * w_sel[..., None], axis=1)
    out = jnp.zeros(xt.shape, jnp.float32)
    for e in range(N_EXPERTS):
        hid = jax.nn.silu(xt @ p['w_e_gate'][e]) * (xt @ p['w_e_up'][e])
        out = out + combine[:, e:e + 1] * (hid @ p['w_e_down'][e]).astype(jnp.float32)
    return out.reshape(B, T, D)


def trunk_layer(x, pos, p, gla_fn, attn_fn):
    B, T, _ = x.shape
    z = x @ p['w_in']
    gq, gk, gv, glr, gr, dq, dk, dv, gates = jnp.split(z, IN_OFFSETS, axis=-1)
    la = jax.nn.log_sigmoid((glr @ p['w_gla_g2'] + p['b_gla_g2']).astype(jnp.float32)) / GLA_TAU
    S_new, o_gla = gla_fn(gq.reshape(B, T, GLA_HEADS, GLA_DK) * (GLA_DK ** -0.5),
                          gk.reshape(B, T, GLA_HEADS, GLA_DK),
                          gv.reshape(B, T, GLA_HEADS, GLA_DV),
                          la.reshape(B, T, GLA_HEADS, GLA_DK))
    o_gla = rms_norm(o_gla, p['gla_norm_g']) * jax.nn.silu(gr.reshape(B, T, GLA_HEADS, GLA_DV).astype(jnp.float32))
    branch_a = o_gla.reshape(B, T, -1).astype(x.dtype) @ p['w_gla_out']
    lam = (jnp.exp(jnp.sum(p['lambda_q1'].astype(jnp.float32) * p['lambda_k1'].astype(jnp.float32)))
           - jnp.exp(jnp.sum(p['lambda_q2'].astype(jnp.float32) * p['lambda_k2'].astype(jnp.float32)))
           + p['lam_init'])
    q = rotary(dq.reshape(B, T, DIFF_HEADS, 2, DIFF_DH), pos)
    k = rotary(dk.reshape(B, T, DIFF_HEADS, 2, DIFF_DH), pos)
    v = dv.reshape(B, T, DIFF_HEADS, DIFF_DV)
    o_diff = attn_fn(q, k, v, lam)
    o_diff = rms_norm(o_diff, p['diff_norm_g']) * (1.0 - p['lam_init'])
    branch_b = o_diff.reshape(B, T, -1).astype(x.dtype) @ p['w_diff_out']
    g_a, g_b = jnp.split(jax.nn.sigmoid((gates + p['b_gates']).astype(jnp.float32)), 2, axis=-1)
    merged = (g_a * branch_a.astype(jnp.float32) + g_b * branch_b.astype(jnp.float32)).astype(x.dtype)
    h = layer_norm(DN_ALPHA * x.astype(jnp.float32) + (merged @ p['w_o']).astype(jnp.float32),
                   p['ln1_g'], p['ln1_b']).astype(x.dtype)
    y = layer_norm(DN_ALPHA * h.astype(jnp.float32) + hier_moe(h, p), p['ln2_g'], p['ln2_b']).astype(x.dtype)
    return y, S_new, k, v


def setup_inputs(seed: int = 0) -> dict:
    key = jax.random.key(seed)
    ks = iter(jax.random.split(key, 40))
    f32 = jnp.float32

    def nrm(shape, scale):
        return jax.random.normal(next(ks), shape, f32) * scale

    n_pages = PAST_LEN // PAGE_SIZE
    n_used = DEC_BATCH * n_pages
    n_pool = n_used + n_used // 4
    x_prompt = nrm((BATCH, SEQ, D_MODEL), 1.0)
    x_sample = nrm((DEC_BATCH, DEC_SEQ, D_MODEL), 1.0)
    cache_k = nrm((DEPTH, n_pool, PAGE_SIZE, DIFF_HEADS, 2, DIFF_DH), 1.0)
    cache_v = nrm((DEPTH, n_pool, PAGE_SIZE, DIFF_HEADS, DIFF_DV), DN_BETA)
    state_gla = nrm((DEPTH, DEC_BATCH, GLA_HEADS, GLA_DK, GLA_DV), 0.5)
    page_table = jax.random.permutation(next(ks), n_pool)[:n_used].reshape(DEC_BATCH, n_pages).astype(jnp.int32)
    col_scale = jnp.concatenate([jnp.full((n,), DN_BETA if s else 1.0, f32)
                                 for n, s in zip(IN_SIZES, IN_BETA_SCALED)])
    w_in = nrm((DEPTH, D_MODEL, D_IN), D_MODEL ** -0.5) * col_scale
    return {
        'x_prompt': x_prompt,
        'x_sample': x_sample,
        'cache_k': cache_k,
        'cache_v': cache_v,
        'state_gla': state_gla,
        'page_table': page_table,
        'w_in': w_in,
        'b_gates': nrm((DEPTH, 2 * D_MODEL), 0.1),
        'w_gla_g2': nrm((DEPTH, GLA_RANK, GLA_HEADS * GLA_DK), GLA_RANK ** -0.5),
        'b_gla_g2': nrm((DEPTH, GLA_HEADS * GLA_DK), 0.1),
        'gla_norm_g': 1.0 + nrm((DEPTH, GLA_DV), 0.1),
        'w_gla_out': nrm((DEPTH, GLA_HEADS * GLA_DV, D_MODEL), (GLA_HEADS * GLA_DV) ** -0.5 * DN_BETA),
        'lambda_q1': nrm((DEPTH, DIFF_DH), 0.1),
        'lambda_k1': nrm((DEPTH, DIFF_DH), 0.1),
        'lambda_q2': nrm((DEPTH, DIFF_DH), 0.1),
        'lambda_k2': nrm((DEPTH, DIFF_DH), 0.1),
        'diff_norm_g': 1.0 + nrm((DEPTH, DIFF_DV), 0.1),
        'w_diff_out': nrm((DEPTH, DIFF_HEADS * DIFF_DV, D_MODEL), (DIFF_HEADS * DIFF_DV) ** -0.5 * DN_BETA),
        'w_o': nrm((DEPTH, D_MODEL, D_MODEL), D_MODEL ** -0.5 * DN_BETA),
        'ln1_g': 1.0 + nrm((DEPTH, D_MODEL), 0.1),
        'ln1_b': nrm((DEPTH, D_MODEL), 0.1),
        'w_group_router': nrm((DEPTH, D_MODEL, N_GROUPS), D_MODEL ** -0.5),
        'b_group_router': nrm((DEPTH, N_GROUPS), 0.01),
        'w_expert_router': nrm((DEPTH, D_MODEL, N_EXPERTS), D_MODEL ** -0.5),
        'b_expert_router': nrm((DEPTH, N_EXPERTS), 0.01),
        'w_e_gate': nrm((DEPTH, N_EXPERTS, D_MODEL, D_EXPERT), D_MODEL ** -0.5),
        'w_e_up': nrm((DEPTH, N_EXPERTS, D_MODEL, D_EXPERT), D_MODEL ** -0.5 * DN_BETA),
        'w_e_down': nrm((DEPTH, N_EXPERTS, D_EXPERT, D_MODEL), D_EXPERT ** -0.5 * DN_BETA),
        'ln2_g': 1.0 + nrm((DEPTH, D_MODEL), 0.1),
        'ln2_b': nrm((DEPTH, D_MODEL), 0.1),
    }


def reference(x_prompt, x_sample, cache_k, cache_v, state_gla, page_table, w_in, b_gates,
              w_gla_g2, b_gla_g2, gla_norm_g, w_gla_out, lambda_q1, lambda_k1, lambda_q2, lambda_k2,
              diff_norm_g, w_diff_out, w_o, ln1_g, ln1_b, w_group_router, b_group_router,
              w_expert_router, b_expert_router, w_e_gate, w_e_up, w_e_down, ln2_g, ln2_b):
    past_len = page_table.shape[1] * PAGE_SIZE
    pos_p = jnp.arange(x_prompt.shape[1], dtype=jnp.float32)
    pos_s = (past_len + jnp.arange(x_sample.shape[1])).astype(jnp.float32)
    xp, xs = x_prompt, x_sample
    kp_l, vp_l, sp_l, ks_l, vs_l, ss_l = [], [], [], [], [], []
    for l in range(DEPTH):
        p = {
            'w_in': w_in[l], 'b_gates': b_gates[l], 'w_gla_g2': w_gla_g2[l], 'b_gla_g2': b_gla_g2[l],
            'gla_norm_g': gla_norm_g[l], 'w_gla_out': w_gla_out[l],
            'lambda_q1': lambda_q1[l], 'lambda_k1': lambda_k1[l],
            'lambda_q2': lambda_q2[l], 'lambda_k2': lambda_k2[l],
            'lam_init': lambda_init(l), 'diff_norm_g': diff_norm_g[l], 'w_diff_out': w_diff_out[l],
            'w_o': w_o[l], 'ln1_g': ln1_g[l], 'ln1_b': ln1_b[l],
            'w_group_router': w_group_router[l], 'b_group_router': b_group_router[l],
            'w_expert_router': w_expert_router[l], 'b_expert_router': b_expert_router[l],
            'w_e_gate': w_e_gate[l], 'w_e_up': w_e_up[l], 'w_e_down': w_e_down[l],
            'ln2_g': ln2_g[l], 'ln2_b': ln2_b[l],
        }
        attn_s = functools.partial(diff_attn_sample, cache_k=cache_k, cache_v=cache_v,
                                   page_table=page_table, layer=l)
        gla_s = functools.partial(gla_sample, state_gla[l])
        xp, sp, kp, vp = trunk_layer(xp, pos_p, p, gla_prompt, diff_attn_prompt)
        xs, ss, kss, vss = trunk_layer(xs, pos_s, p, gla_s, attn_s)
        kp_l.append(kp.astype(cache_k.dtype))
        vp_l.append(vp.astype(cache_v.dtype))
        sp_l.append(sp.astype(state_gla.dtype))
        ks_l.append(kss.astype(cache_k.dtype))
        vs_l.append(vss.astype(cache_v.dtype))
        ss_l.append(ss.astype(state_gla.dtype))
    k_prompt = jnp.stack(kp_l)
    v_prompt = jnp.stack(vp_l)
    gla_prompt_state = jnp.stack(sp_l)
    k_sample = jnp.stack(ks_l)
    v_sample = jnp.stack(vs_l)
    gla_sample_state = jnp.stack(ss_l)
    return (xp, xs, k_prompt, v_prompt, gla_prompt_state, k_sample, v_sample, gla_sample_state)
```

```python
import functools
import math

import jax
import jax.numpy as jnp
from jax import lax
from jax.experimental import pallas as pl
from jax.experimental.pallas import tpu as pltpu

F32 = jnp.float32
BF16 = jnp.bfloat16

GLA_TAU = 16.0
ROPE_THETA = 500000.0
LN_EPS = 1e-5
RMS_EPS = 1e-5
LANES = 128
VMEM_LIMIT = 56 * 1024 * 1024
NEG_BIG = -1e30


def _lambda_init(layer):
    return 0.8 - 0.6 * math.exp(-0.3 * layer)


def _sigmoid(x):
    return 1.0 / (1.0 + jnp.exp(-x))


def _dot(a, b):
    return jnp.dot(a, b, preferred_element_type=F32)


def _dot_nt(a, b):
    return lax.dot_general(a, b, (((1,), (1,)), ((), ())), preferred_element_type=F32)


def _dot_tn(a, b):
    return lax.dot_general(a, b, (((0,), (0,)), ((), ())), preferred_element_type=F32)


def _params(*sem):
    return pltpu.CompilerParams(dimension_semantics=sem, vmem_limit_bytes=VMEM_LIMIT)


def _in_proj_kernel(x_ref, w_ref, wglr_ref, wg2_ref, bg2_ref, cos_ref, sa_ref, sb_ref,
                    z_ref, la_ref, *, rot_blocks, rot_dim):
    j = pl.program_id(1)
    xb = x_ref[...].astype(BF16)
    acc = _dot(xb, w_ref[...])
    z_ref[...] = acc

    is_rot = functools.reduce(jnp.logical_or, [j == r for r in rot_blocks])

    @pl.when(is_rot)
    def _():
        half = rot_dim // 2
        cos, sa, sb = cos_ref[...], sa_ref[...], sb_ref[...]
        for g in range(acc.shape[1] // LANES):
            xg = acc[:, g * LANES:(g + 1) * LANES]
            up = pltpu.roll(xg, LANES - half, 1)
            dn = pltpu.roll(xg, half, 1)
            z_ref[:, g * LANES:(g + 1) * LANES] = xg * cos + up * sa + dn * sb

    @pl.when(j == 0)
    def _():
        glr = _dot(xb, wglr_ref[...])
        logit = _dot(glr.astype(BF16), wg2_ref[...]) + bg2_ref[...]
        ls = jnp.minimum(logit, 0.0) - jnp.log(1.0 + jnp.exp(-jnp.abs(logit)))
        la_ref[...] = ls * (1.0 / GLA_TAU)


def _in_proj(x, w_main, w_glr, w_g2, b_g2, cos_t, sa_t, sb_t, *, tm, rot_blocks, rot_dim):
    n, d = x.shape
    n_col = w_main.shape[1] // d
    tab_blocks = cos_t.shape[0] // tm
    rank = w_glr.shape[1]
    dla = w_g2.shape[1]
    tab_spec = pl.BlockSpec((tm, LANES), lambda i, j: (i % tab_blocks, 0))
    return pl.pallas_call(
        functools.partial(_in_proj_kernel, rot_blocks=rot_blocks, rot_dim=rot_dim),
        grid=(n // tm, n_col),
        in_specs=[
            pl.BlockSpec((tm, d), lambda i, j: (i, 0)),
            pl.BlockSpec((d, d), lambda i, j: (0, j)),
            pl.BlockSpec((d, rank), lambda i, j: (0, 0)),
            pl.BlockSpec((rank, dla), lambda i, j: (0, 0)),
            pl.BlockSpec((1, dla), lambda i, j: (0, 0)),
            tab_spec, tab_spec, tab_spec,
        ],
        out_specs=[
            pl.BlockSpec((tm, d), lambda i, j: (i, j)),
            pl.BlockSpec((tm, dla), lambda i, j: (i, 0)),
        ],
        out_shape=[jax.ShapeDtypeStruct((n, w_main.shape[1]), F32),
                   jax.ShapeDtypeStruct((n, dla), F32)],
        compiler_params=_params("parallel", "arbitrary"),
        name="in_proj",
    )(x, w_main, w_glr, w_g2, b_g2, cos_t, sa_t, sb_t)


def _gla_kernel(q_ref, k_ref, v_ref, la_ref, r_ref, s0_ref, g_ref, o_ref, s_ref, *, heads, dk, dv):
    c = pl.program_id(1)

    @pl.when(c == 0)
    def _():
        s_ref[...] = s0_ref[...]

    la = la_ref[0]
    ct = la.shape[0]
    row = lax.broadcasted_iota(jnp.int32, (ct, ct), 0)
    col = lax.broadcasted_iota(jnp.int32, (ct, ct), 1)
    causal = col <= row
    tri = causal.astype(F32)
    b = jnp.dot(tri, la, preferred_element_type=F32, precision=lax.Precision.HIGHEST)
    b_last = b[ct - 1:ct, :]
    q_t = q_ref[0] * (dk ** -0.5) * jnp.exp(b)
    k_t = k_ref[0] * jnp.exp(-b)
    k_d = k_ref[0] * jnp.exp(b_last - b)
    d_row = jnp.exp(b_last)
    eye = (lax.broadcasted_iota(jnp.int32, (dk, dk), 0) == lax.broadcasted_iota(jnp.int32, (dk, dk), 1))
    gain = g_ref[...]
    op = (lambda a: a.astype(BF16)) if ct >= 16 else (lambda a: a)
    for h in range(heads):
        ks = slice(h * dk, (h + 1) * dk)
        vs = slice(h * dv, (h + 1) * dv)
        qh = op(q_t[:, ks])
        vh = op(v_ref[0, :, vs])
        s_old = s_ref[0, h]
        scores = jnp.where(causal, _dot_nt(qh, op(k_t[:, ks])), 0.0)
        o = _dot(op(scores), vh) + _dot(qh, op(s_old))
        d_col = jnp.sum(jnp.where(eye, jnp.broadcast_to(d_row[:, ks], (dk, dk)), 0.0), axis=1, keepdims=True)
        s_ref[0, h] = d_col * s_old + _dot_tn(op(k_d[:, ks]), vh)
        o = o * lax.rsqrt(jnp.mean(o * o, axis=-1, keepdims=True) + RMS_EPS) * gain
        r = r_ref[0, :, vs]
        o_ref[0, :, vs] = o * (r * _sigmoid(r))


def _gla(z, la, s0, gain, *, chunk, heads, dk, dv):
    bsz, t, _ = z.shape
    hk, hv = heads * dk, heads * dv
    return pl.pallas_call(
        functools.partial(_gla_kernel, heads=heads, dk=dk, dv=dv),
        grid=(bsz, t // chunk),
        in_specs=[
            pl.BlockSpec((1, chunk, hk), lambda b, c: (b, c, 0)),
            pl.BlockSpec((1, chunk, hk), lambda b, c: (b, c, 1)),
            pl.BlockSpec((1, chunk, hv), lambda b, c: (b, c, 1)),
            pl.BlockSpec((1, chunk, hk), lambda b, c: (b, c, 0)),
            pl.BlockSpec((1, chunk, hv), lambda b, c: (b, c, 2)),
            pl.BlockSpec((1, heads, dk, dv), lambda b, c: (b, 0, 0, 0)),
            pl.BlockSpec((1, dv), lambda b, c: (0, 0)),
        ],
        out_specs=[
            pl.BlockSpec((1, chunk, hv), lambda b, c: (b, c, 0)),
            pl.BlockSpec((1, heads, dk, dv), lambda b, c: (b, 0, 0, 0)),
        ],
        out_shape=[jax.ShapeDtypeStruct((bsz, t, hv), F32),
                   jax.ShapeDtypeStruct((bsz, heads, dk, dv), F32)],
        compiler_params=_params("parallel", "arbitrary"),
        name="gla",
    )(z, z, z, la, z, s0, gain)


def _flash_kernel(lam_ref, q_ref, k_ref, v_ref, g_ref, o_ref, m_ref, l_ref, acc_ref, *, blk, dh, out_scale):
    qi = pl.program_id(2)
    m_ref[...] = jnp.full(m_ref.shape, NEG_BIG, F32)
    l_ref[...] = jnp.zeros(l_ref.shape, F32)
    acc_ref[...] = jnp.zeros(acc_ref.shape, F32)

    def step(kv, masked):
        start = pl.multiple_of(kv * blk, blk)
        vb = v_ref[0, pl.ds(start, blk), :]
        for c in range(2):
            qc = q_ref[0, :, c * dh:(c + 1) * dh]
            kc = k_ref[0, pl.ds(start, blk), c * dh:(c + 1) * dh]
            s = _dot_nt(qc, kc)
            if masked:
                row = lax.broadcasted_iota(jnp.int32, s.shape, 0)
                col = lax.broadcasted_iota(jnp.int32, s.shape, 1)
                s = jnp.where(col <= row, s, NEG_BIG)
            m_old = m_ref[c]
            m_new = jnp.maximum(m_old, jnp.max(s, axis=-1, keepdims=True))
            p = jnp.exp(s - m_new)
            alpha = jnp.exp(m_old - m_new)
            l_ref[c] = alpha * l_ref[c] + jnp.sum(p, axis=-1, keepdims=True)
            acc_ref[c] = alpha * acc_ref[c] + _dot(p.astype(BF16), vb)
            m_ref[c] = m_new

    def body(kv, carry):
        step(kv, False)
        return carry

    lax.fori_loop(0, qi, body, 0)
    step(qi, True)

    lam = lam_ref[0]
    o = acc_ref[0] / l_ref[0] - lam * (acc_ref[1] / l_ref[1])
    o = o * lax.rsqrt(jnp.mean(o * o, axis=-1, keepdims=True) + RMS_EPS) * g_ref[...]
    o_ref[0] = o * out_scale


def _flash(lam, q, k, v, gain, *, blk, heads, dh, dv, out_scale):
    bsz, t, _ = q.shape
    return pl.pallas_call(
        functools.partial(_flash_kernel, blk=blk, dh=dh, out_scale=out_scale),
        grid=(bsz, heads, t // blk),
        in_specs=[
            pl.BlockSpec(memory_space=pltpu.SMEM),
            pl.BlockSpec((1, blk, 2 * dh), lambda b, h, i: (b, i, h)),
            pl.BlockSpec((1, t, 2 * dh), lambda b, h, i: (b, 0, h)),
            pl.BlockSpec((1, t, dv), lambda b, h, i: (b, 0, h)),
            pl.BlockSpec((1, dv), lambda b, h, i: (0, 0)),
        ],
        out_specs=pl.BlockSpec((1, blk, dv), lambda b, h, i: (b, i, h)),
        out_shape=jax.ShapeDtypeStruct((bsz, t, heads * dv), F32),
        scratch_shapes=[pltpu.VMEM((2, blk, 1), F32), pltpu.VMEM((2, blk, 1), F32),
                        pltpu.VMEM((2, blk, dv), F32)],
        compiler_params=_params("parallel", "parallel", "arbitrary"),
        name="flash_diff_attn",
    )(lam, q, k, v, gain)


def _decode_kernel(pt_ref, lam_ref, q_ref, kn_ref, vn_ref, g_ref, *rest, pages, page, heads, dv, tpad,
                   n_new, out_scale):
    k_refs = rest[:pages]
    v_refs = rest[pages:2 * pages]
    o_ref = rest[2 * pages]
    m_ref, l_ref, acc_ref = rest[2 * pages + 1:]
    c = pl.program_id(1)

    @pl.when(c == 0)
    def _():
        m_ref[...] = jnp.full(m_ref.shape, NEG_BIG, F32)
        l_ref[...] = jnp.zeros(l_ref.shape, F32)
        acc_ref[...] = jnp.zeros(acc_ref.shape, F32)

    qf = q_ref[0]
    qb = qf.astype(BF16)

    def update(s, pv_fn):
        m_old = m_ref[...]
        m_new = jnp.maximum(m_old, jnp.max(s, axis=-1, keepdims=True))
        p = jnp.exp(s - m_new)
        alpha = jnp.exp(m_old - m_new)
        l_ref[...] = alpha * l_ref[...] + jnp.sum(p, axis=-1, keepdims=True)
        acc_ref[...] = alpha * acc_ref[...] + pv_fn(p)
        m_ref[...] = m_new

    s = jnp.concatenate([_dot_nt(qb, k_refs[i][0].astype(BF16)) for i in range(pages)], axis=1)

    def pv_pages(p):
        p = p.astype(BF16)
        out = _dot(p[:, 0:page], v_refs[0][0].astype(BF16))
        for i in range(1, pages):
            out = out + _dot(p[:, i * page:(i + 1) * page], v_refs[i][0].astype(BF16))
        return out

    update(s, pv_pages)

    @pl.when(c == pl.num_programs(1) - 1)
    def _():
        sn = _dot_nt(qf, kn_ref[0])
        tok = lax.broadcasted_iota(jnp.int32, sn.shape, 0) % tpad
        key = lax.broadcasted_iota(jnp.int32, sn.shape, 1)
        sn = jnp.where((key <= tok) & (key < n_new), sn, NEG_BIG)
        update(sn, lambda p: _dot(p, vn_ref[0]))
        half = heads * tpad
        lam = lam_ref[0]
        o = acc_ref[0:half, :] / l_ref[0:half, :] - lam * (acc_ref[half:, :] / l_ref[half:, :])
        for h in range(heads):
            oh = o[h * tpad:(h + 1) * tpad, h * dv:(h + 1) * dv]
            oh = oh * lax.rsqrt(jnp.mean(oh * oh, axis=-1, keepdims=True) + RMS_EPS) * g_ref[...]
            o_ref[0, :, h * dv:(h + 1) * dv] = oh * out_scale


def _decode(page_table, lam, qblk, k_new, v_new, gain, cache_k, cache_v, *, pages, heads, dv, tpad,
            n_new, out_scale):
    dbsz, rows, d = qblk.shape
    page = cache_k.shape[1]
    n_chunks = page_table.shape[1] // pages

    def page_spec(i):
        return pl.BlockSpec((1, page, d), lambda b, c, pt: (pt[b, c * pages + i], 0, 0))

    return pl.pallas_call(
        functools.partial(_decode_kernel, pages=pages, page=page, heads=heads, dv=dv, tpad=tpad,
                          n_new=n_new, out_scale=out_scale),
        grid_spec=pltpu.PrefetchScalarGridSpec(
            num_scalar_prefetch=1,
            grid=(dbsz, n_chunks),
            in_specs=[
                pl.BlockSpec(memory_space=pltpu.SMEM),
                pl.BlockSpec((1, rows, d), lambda b, c, pt: (b, 0, 0)),
                pl.BlockSpec((1, tpad, d), lambda b, c, pt: (b, 0, 0)),
                pl.BlockSpec((1, tpad, d), lambda b, c, pt: (b, 0, 0)),
                pl.BlockSpec((1, dv), lambda b, c, pt: (0, 0)),
            ] + [page_spec(i) for i in range(pages)] + [page_spec(i) for i in range(pages)],
            out_specs=pl.BlockSpec((1, tpad, d), lambda b, c, pt: (b, 0, 0)),
            scratch_shapes=[pltpu.VMEM((rows, 1), F32), pltpu.VMEM((rows, 1), F32),
                            pltpu.VMEM((rows, d), F32)],
        ),
        out_shape=jax.ShapeDtypeStruct((dbsz, tpad, d), F32),
        compiler_params=_params("parallel", "arbitrary"),
        name="paged_diff_attn",
    )(page_table, lam, qblk, k_new, v_new, gain, *([cache_k] * pages), *([cache_v] * pages))


def _merge_kernel(x_ref, oa_ref, ob_ref, ga_ref, gb_ref, bga_ref, bgb_ref, wa_ref, wb_ref, wo_ref,
                  lg_ref, lb_ref, wr_ref, br_ref, h_ref, comb_ref, *, alpha, n_groups, per_group):
    a = _dot(oa_ref[...].astype(BF16), wa_ref[...])
    b = _dot(ob_ref[...].astype(BF16), wb_ref[...])
    g_a = _sigmoid(ga_ref[...] + bga_ref[...])
    g_b = _sigmoid(gb_ref[...] + bgb_ref[...])
    merged = g_a * a + g_b * b
    u = alpha * x_ref[...] + _dot(merged.astype(BF16), wo_ref[...])
    mu = jnp.mean(u, axis=-1, keepdims=True)
    var = jnp.mean(jnp.square(u - mu), axis=-1, keepdims=True)
    h = (u - mu) * lax.rsqrt(var + LN_EPS) * lg_ref[...] + lb_ref[...]
    h_ref[...] = h

    n_exp = n_groups * per_group
    logits = _dot(h.astype(BF16), wr_ref[...]) + br_ref[...]
    lg = logits[:, 0:n_groups]
    el = logits[:, n_groups:n_groups + n_exp]
    gidx = lax.broadcasted_iota(jnp.int32, lg.shape, 1)
    gmax = jnp.max(lg, axis=-1, keepdims=True)
    g_sel = jnp.min(jnp.where(lg == gmax, gidx, n_groups), axis=-1, keepdims=True)
    g_w = 1.0 / jnp.sum(jnp.exp(lg - gmax), axis=-1, keepdims=True)
    eidx = lax.broadcasted_iota(jnp.int32, el.shape, 1)
    in_group = (eidx >= g_sel * per_group) & (eidx < (g_sel + 1) * per_group)
    e1 = jnp.where(in_group, el, -jnp.inf)
    v1 = jnp.max(e1, axis=-1, keepdims=True)
    i1 = jnp.min(jnp.where(e1 == v1, eidx, n_exp), axis=-1, keepdims=True)
    e2 = jnp.where(eidx == i1, -jnp.inf, e1)
    v2 = jnp.max(e2, axis=-1, keepdims=True)
    i2 = jnp.min(jnp.where(e2 == v2, eidx, n_exp), axis=-1, keepdims=True)
    t = jnp.exp(v2 - v1)
    w1 = g_w / (1.0 + t)
    w2 = g_w * t / (1.0 + t)
    comb_ref[...] = jnp.where(eidx == i1, w1, 0.0) + jnp.where(eidx == i2, w2, 0.0)


def _merge(x, o_gla, o_diff, z, b_ga, b_gb, w_a, w_b, w_o, ln_g, ln_b, w_r, b_r, *, tm, alpha,
           n_groups, per_group, gate_block):
    n, d = x.shape
    n_exp = n_groups * per_group
    row = lambda i: (i, 0)
    fixed = lambda i: (0, 0)
    return pl.pallas_call(
        functools.partial(_merge_kernel, alpha=alpha, n_groups=n_groups, per_group=per_group),
        grid=(n // tm,),
        in_specs=[
            pl.BlockSpec((tm, d), row), pl.BlockSpec((tm, d), row), pl.BlockSpec((tm, d), row),
            pl.BlockSpec((tm, d), lambda i: (i, gate_block)),
            pl.BlockSpec((tm, d), lambda i: (i, gate_block + 1)),
            pl.BlockSpec((1, d), fixed), pl.BlockSpec((1, d), fixed),
            pl.BlockSpec((d, d), fixed), pl.BlockSpec((d, d), fixed), pl.BlockSpec((d, d), fixed),
            pl.BlockSpec((1, d), fixed), pl.BlockSpec((1, d), fixed),
            pl.BlockSpec((d, n_groups + n_exp), fixed), pl.BlockSpec((1, n_groups + n_exp), fixed),
        ],
        out_specs=[pl.BlockSpec((tm, d), row), pl.BlockSpec((tm, n_exp), row)],
        out_shape=[jax.ShapeDtypeStruct((n, d), F32), jax.ShapeDtypeStruct((n, n_exp), F32)],
        compiler_params=_params("parallel"),
        name="merge_ln_route",
    )(x, o_gla, o_diff, z, z, b_ga, b_gb, w_a, w_b, w_o, ln_g, ln_b, w_r, b_r)


def _moe_kernel(h_ref, comb_ref, wg_ref, wu_ref, wd_ref, lg_ref, lb_ref, y_ref, acc_ref, *, alpha):
    e = pl.program_id(1)

    @pl.when(e == 0)
    def _():
        acc_ref[...] = jnp.zeros(acc_ref.shape, F32)

    hb = h_ref[...].astype(BF16)
    gate = _dot(hb, wg_ref[0])
    up = _dot(hb, wu_ref[0])
    hid = gate * _sigmoid(gate) * up
    comb = comb_ref[...]
    eidx = lax.broadcasted_iota(jnp.int32, comb.shape, 1)
    w = jnp.sum(jnp.where(eidx == e, comb, 0.0), axis=-1, keepdims=True)
    acc_ref[...] += w * _dot(hid.astype(BF16), wd_ref[0])

    @pl.when(e == pl.num_programs(1) - 1)
    def _():
        u = alpha * h_ref[...] + acc_ref[...]
        mu = jnp.mean(u, axis=-1, keepdims=True)
        var = jnp.mean(jnp.square(u - mu), axis=-1, keepdims=True)
        y_ref[...] = (u - mu) * lax.rsqrt(var + LN_EPS) * lg_ref[...] + lb_ref[...]


def _moe(h, comb, w_gate, w_up, w_down, ln_g, ln_b, *, tm, alpha):
    n, d = h.shape
    n_exp, _, d_exp = w_gate.shape
    return pl.pallas_call(
        functools.partial(_moe_kernel, alpha=alpha),
        grid=(n // tm, n_exp),
        in_specs=[
            pl.BlockSpec((tm, d), lambda i, e: (i, 0)),
            pl.BlockSpec((tm, n_exp), lambda i, e: (i, 0)),
            pl.BlockSpec((1, d, d_exp), lambda i, e: (e, 0, 0)),
            pl.BlockSpec((1, d, d_exp), lambda i, e: (e, 0, 0)),
            pl.BlockSpec((1, d_exp, d), lambda i, e: (e, 0, 0)),
            pl.BlockSpec((1, d), lambda i, e: (0, 0)),
            pl.BlockSpec((1, d), lambda i, e: (0, 0)),
        ],
        out_specs=pl.BlockSpec((tm, d), lambda i, e: (i, 0)),
        out_shape=jax.ShapeDtypeStruct((n, d), F32),
        scratch_shapes=[pltpu.VMEM((tm, d), F32)],
        compiler_params=_params("parallel", "arbitrary"),
        name="moe_ln",
    )(h, comb, w_gate, w_up, w_down, ln_g, ln_b)


def _rotary_tables(pos, rot_dim, reps):
    half = rot_dim // 2
    inv_freq = ROPE_THETA ** (-jnp.arange(0, rot_dim, 2, dtype=F32) / rot_dim)
    ang = pos[:, None] * inv_freq[None, :]
    cos, sin = jnp.cos(ang), jnp.sin(ang)
    t = pos.shape[0]
    zeros = jnp.zeros((t, LANES - rot_dim), F32)
    zh = jnp.zeros((t, half), F32)
    cos_t = jnp.concatenate([cos, cos, jnp.ones((t, LANES - rot_dim), F32)], axis=1)
    sa_t = jnp.concatenate([-sin, zh, zeros], axis=1)
    sb_t = jnp.concatenate([zh, sin, zeros], axis=1)
    return tuple(jnp.tile(a, (reps, 1)) for a in (cos_t, sa_t, sb_t))


def _pick(n, pref):
    t = min(n, pref)
    while n % t:
        t //= 2
    return t


def kernel(x_prompt, x_sample, cache_k, cache_v, state_gla, page_table, w_in, b_gates, w_gla_g2, b_gla_g2,
           gla_norm_g, w_gla_out, lambda_q1, lambda_k1, lambda_q2, lambda_k2, diff_norm_g, w_diff_out, w_o,
           ln1_g, ln1_b, w_group_router, b_group_router, w_expert_router, b_expert_router, w_e_gate, w_e_up,
           w_e_down, ln2_g, ln2_b):
    bsz, seq, d = x_prompt.shape
    dbsz, dseq, _ = x_sample.shape
    depth = w_in.shape[0]
    _, n_pool, page, heads, _, dh = cache_k.shape
    dv = cache_v.shape[-1]
    _, _, gheads, gdk, gdv = state_gla.shape
    rank = w_gla_g2.shape[1]
    n_groups = w_group_router.shape[-1]
    n_exp = w_expert_router.shape[-1]
    per_group = n_exp // n_groups
    rot_dim = dh // 4
    alpha = (2.0 * depth) ** 0.25
    past_len = page_table.shape[1] * page
    hk = gheads * gdk
    assert hk * 2 == d and gheads * gdv == d and heads * 2 * dh == d and heads * dv == d
    off_glr = 2 * hk + gheads * gdv

    np_tok, ns_tok = bsz * seq, dbsz * dseq
    tpad = 8
    assert dseq <= tpad
    tm_p = _pick(np_tok, 1024)
    tm_s = _pick(ns_tok, 512)
    chunk = _pick(seq, 128)
    blk = _pick(seq, 256)
    pages = _pick(page_table.shape[1], 8)

    pos_p = jnp.arange(seq, dtype=F32)
    pos_s = (past_len + jnp.arange(dseq)).astype(F32)
    tabs_p = _rotary_tables(pos_p, rot_dim, max(1, tm_p // seq))
    tabs_s = _rotary_tables(pos_s, rot_dim, tm_s // dseq)

    xp = x_prompt.reshape(np_tok, d)
    xs = x_sample.reshape(ns_tok, d)
    outs = [[] for _ in range(6)]
    for l in range(depth):
        lam_init = _lambda_init(l)
        w = w_in[l]
        w_main = jnp.concatenate([w[:, :off_glr], w[:, off_glr + rank:]], axis=1).astype(BF16)
        w_glr = w[:, off_glr:off_glr + rank].astype(BF16)
        w_g2 = w_gla_g2[l].astype(BF16)
        b_g2 = b_gla_g2[l].reshape(1, hk)
        b_ga = b_gates[l, :d].reshape(1, d)
        b_gb = b_gates[l, d:].reshape(1, d)
        w_a, w_b, w_out = (t[l].astype(BF16) for t in (w_gla_out, w_diff_out, w_o))
        w_r = jnp.concatenate([w_group_router[l], w_expert_router[l]], axis=1).astype(BF16)
        b_r = jnp.concatenate([b_group_router[l], b_expert_router[l]]).reshape(1, n_groups + n_exp)
        w_eg, w_eu, w_ed = (t[l].astype(BF16) for t in (w_e_gate, w_e_up, w_e_down))
        g_gla = gla_norm_g[l].reshape(1, gdv)
        g_diff = diff_norm_g[l].reshape(1, dv)
        ln1 = (ln1_g[l].reshape(1, d), ln1_b[l].reshape(1, d))
        ln2 = (ln2_g[l].reshape(1, d), ln2_b[l].reshape(1, d))
        lam = (jnp.exp(jnp.sum(lambda_q1[l] * lambda_k1[l])) - jnp.exp(jnp.sum(lambda_q2[l] * lambda_k2[l]))
               + lam_init).astype(F32).reshape(1)
        q_scale = dh ** -0.5

        def tail(x, o_gla, o_diff, z, tm):
            h, comb = _merge(x, o_gla, o_diff, z, b_ga, b_gb, w_a, w_b, w_out, *ln1, w_r, b_r, tm=min(tm, 512),
                             alpha=alpha, n_groups=n_groups, per_group=per_group, gate_block=6)
            return _moe(h, comb, w_eg, w_eu, w_ed, *ln2, tm=tm, alpha=alpha)

        z, la = _in_proj(xp, w_main, w_glr, w_g2, b_g2, *tabs_p, tm=tm_p, rot_blocks=(3, 4), rot_dim=rot_dim)
        z3 = z.reshape(bsz, seq, 8 * d)
        s0 = jnp.zeros((bsz, gheads, gdk, gdv), F32)
        o_gla, s_p = _gla(z3, la.reshape(bsz, seq, hk), s0, g_gla, chunk=chunk, heads=gheads, dk=gdk, dv=gdv)
        k_p = z3[:, :, 4 * d:5 * d]
        v_p = z3[:, :, 5 * d:6 * d]
        q_b = (z3[:, :, 3 * d:4 * d] * q_scale).astype(BF16)
        o_diff = _flash(lam, q_b, k_p.astype(BF16), v_p.astype(BF16), g_diff, blk=blk, heads=heads, dh=dh,
                        dv=dv, out_scale=1.0 - lam_init)
        xp = tail(xp, o_gla.reshape(np_tok, d), o_diff.reshape(np_tok, d), z, tm_p)
        outs[0].append(k_p.reshape(bsz, seq, heads, 2, dh))
        outs[1].append(v_p.reshape(bsz, seq, heads, dv))
        outs[2].append(s_p)

        z, la = _in_proj(xs, w_main, w_glr, w_g2, b_g2, *tabs_s, tm=tm_s, rot_blocks=(3, 4), rot_dim=rot_dim)
        pad = ((0, 0), (0, tpad - dseq), (0, 0))
        z3 = jnp.pad(z.reshape(dbsz, dseq, 8 * d), pad)
        la3 = jnp.pad(la.reshape(dbsz, dseq, hk), pad)
        o_gla, s_s = _gla(z3, la3, state_gla[l], g_gla, chunk=tpad, heads=gheads, dk=gdk, dv=gdv)
        k_s = z3[:, :, 4 * d:5 * d]
        v_s = z3[:, :, 5 * d:6 * d]
        q5 = (z3[:, :, 3 * d:4 * d] * q_scale).reshape(dbsz, tpad, heads, 2, dh)
        q5 = jnp.transpose(q5, (0, 3, 2, 1, 4))
        same_c = jnp.arange(2)[:, None, None, None] == jnp.arange(2)[None, None, None, :]
        same_h = jnp.arange(heads)[None, :, None, None] == jnp.arange(heads)[None, None, :, None]
        sel = (same_c & same_h)[None, :, :, None, :, :, None]
        qblk = jnp.where(sel, q5[:, :, :, :, None, None, :], 0.0)
        qblk = qblk.reshape(dbsz, 2 * heads * tpad, d)
        o_diff = _decode(page_table, lam, qblk, k_s, v_s, g_diff,
                         cache_k[l].reshape(n_pool, page, d), cache_v[l].reshape(n_pool, page, d),
                         pages=pages, heads=heads, dv=dv, tpad=tpad, n_new=dseq, out_scale=1.0 - lam_init)
        xs = tail(xs, o_gla[:, :dseq].reshape(ns_tok, d), o_diff[:, :dseq].reshape(ns_tok, d), z, tm_s)
        outs[3].append(k_s[:, :dseq].reshape(dbsz, dseq, heads, 2, dh))
        outs[4].append(v_s[:, :dseq].reshape(dbsz, dseq, heads, dv))
        outs[5].append(s_s)

    k_prompt, v_prompt, s_prompt, k_sample, v_sample, s_sample = (jnp.stack(o) for o in outs)
    return (xp.reshape(bsz, seq, d), xs.reshape(dbsz, dseq, d), k_prompt, v_prompt, s_prompt,
            k_sample, v_sample, s_sample)
```

```python
import functools
import math

import jax
import jax.numpy as jnp
from jax import lax
from jax.experimental import pallas as pl
from jax.experimental.pallas import tpu as pltpu

F32 = jnp.float32
BF16 = jnp.bfloat16

GLA_TAU = 16.0
ROPE_THETA = 500000.0
LN_EPS = 1e-5
RMS_EPS = 1e-5
LOG2E = 1.4426950408889634
LANES = 128
SUBLANES = 8
VMEM_LIMIT = 56 * 1024 * 1024
NEG_BIG = -1e30


def _lambda_init(layer):
    return 0.8 - 0.6 * math.exp(-0.3 * layer)


def _sigmoid(x):
    return 1.0 / (1.0 + jnp.exp(-x))


def _dot(a, b):
    return jnp.dot(a, b, preferred_element_type=F32)


def _dot_nt(a, b):
    return lax.dot_general(a, b, (((1,), (1,)), ((), ())), preferred_element_type=F32)


def _dot_tn(a, b):
    return lax.dot_general(a, b, (((0,), (0,)), ((), ())), preferred_element_type=F32)


def _params(*sem):
    return pltpu.CompilerParams(dimension_semantics=sem, vmem_limit_bytes=VMEM_LIMIT)


def _in_proj_kernel(x_ref, w_ref, wglr_ref, wg2_ref, bg2_ref, cos_ref, sa_ref, sb_ref,
                    z_ref, la_ref, qb_ref, k_ref, kb_ref, v_ref, vb_ref, *, n_plain, rot_dim, q_scale):
    j = pl.program_id(1)
    xb = x_ref[...].astype(BF16)
    acc = _dot(xb, w_ref[...])

    def rotary(a):
        half = rot_dim // 2
        cos, sa, sb = cos_ref[...], sa_ref[...], sb_ref[...]
        out = []
        for g in range(a.shape[1] // LANES):
            xg = a[:, g * LANES:(g + 1) * LANES]
            up = pltpu.roll(xg, LANES - half, 1)
            dn = pltpu.roll(xg, half, 1)
            out.append(xg * cos + up * sa + dn * sb)
        return jnp.concatenate(out, axis=1)

    @pl.when(j < n_plain)
    def _():
        z_ref[...] = acc

    @pl.when(j == n_plain)
    def _():
        qb_ref[...] = (rotary(acc) * q_scale).astype(BF16)

    @pl.when(j == n_plain + 1)
    def _():
        r = rotary(acc)
        k_ref[...] = r
        kb_ref[...] = r.astype(BF16)

    @pl.when(j == n_plain + 2)
    def _():
        v_ref[...] = acc
        vb_ref[...] = acc.astype(BF16)

    @pl.when(j == 0)
    def _():
        glr = _dot(xb, wglr_ref[...])
        logit = _dot(glr.astype(BF16), wg2_ref[...]) + bg2_ref[...]
        ls = jnp.minimum(logit, 0.0) - jnp.log(1.0 + jnp.exp(-jnp.abs(logit)))
        la_ref[...] = ls * (1.0 / GLA_TAU)


def _in_proj(x, w_main, w_glr, w_g2, b_g2, cos_t, sa_t, sb_t, *, tm, rot_dim, q_scale):
    n, d = x.shape
    n_col = w_main.shape[1] // d
    n_plain = n_col - 3
    tab_blocks = cos_t.shape[0] // tm
    rank = w_glr.shape[1]
    dla = w_g2.shape[1]
    tab_spec = pl.BlockSpec((tm, LANES), lambda i, j: (i % tab_blocks, 0))
    row_spec = pl.BlockSpec((tm, d), lambda i, j: (i, 0))
    return pl.pallas_call(
        functools.partial(_in_proj_kernel, n_plain=n_plain, rot_dim=rot_dim, q_scale=q_scale),
        grid=(n // tm, n_col),
        in_specs=[
            row_spec,
            pl.BlockSpec((d, d), lambda i, j: (0, j)),
            pl.BlockSpec((d, rank), lambda i, j: (0, 0)),
            pl.BlockSpec((rank, dla), lambda i, j: (0, 0)),
            pl.BlockSpec((1, dla), lambda i, j: (0, 0)),
            tab_spec, tab_spec, tab_spec,
        ],
        out_specs=[
            pl.BlockSpec((tm, d), lambda i, j: (i, jnp.minimum(j, n_plain - 1))),
            pl.BlockSpec((tm, dla), lambda i, j: (i, 0)),
            row_spec, row_spec, row_spec, row_spec, row_spec,
        ],
        out_shape=[jax.ShapeDtypeStruct((n, n_plain * d), F32),
                   jax.ShapeDtypeStruct((n, dla), F32),
                   jax.ShapeDtypeStruct((n, d), BF16),
                   jax.ShapeDtypeStruct((n, d), F32), jax.ShapeDtypeStruct((n, d), BF16),
                   jax.ShapeDtypeStruct((n, d), F32), jax.ShapeDtypeStruct((n, d), BF16)],
        compiler_params=_params("parallel", "arbitrary"),
        name="in_proj",
    )(x, w_main, w_glr, w_g2, b_g2, cos_t, sa_t, sb_t)


def _gla_kernel(q_ref, k_ref, v_ref, la_ref, r_ref, s0_ref, g_ref, o_ref, s_ref, *, heads, dk, dv):
    c = pl.program_id(1)

    @pl.when(c == 0)
    def _():
        s_ref[...] = s0_ref[...]

    la = la_ref[0]
    ct = la.shape[0]
    row = lax.broadcasted_iota(jnp.int32, (ct, ct), 0)
    col = lax.broadcasted_iota(jnp.int32, (ct, ct), 1)
    causal = col <= row
    tri = causal.astype(F32)
    b = jnp.dot(tri, la, preferred_element_type=F32, precision=lax.Precision.HIGHEST)
    b_last = b[ct - 1:ct, :]
    q_t = q_ref[0] * (dk ** -0.5) * jnp.exp(b)
    k_t = k_ref[0] * jnp.exp(-b)
    k_d = k_ref[0] * jnp.exp(b_last - b)
    d_row = jnp.exp(b_last)
    eye = (lax.broadcasted_iota(jnp.int32, (dk, dk), 0) == lax.broadcasted_iota(jnp.int32, (dk, dk), 1))
    gain = g_ref[...]
    op = (lambda a: a.astype(BF16)) if ct >= 16 else (lambda a: a)
    for h in range(heads):
        ks = slice(h * dk, (h + 1) * dk)
        vs = slice(h * dv, (h + 1) * dv)
        qh = op(q_t[:, ks])
        vh = op(v_ref[0, :, vs])
        s_old = s_ref[0, h]
        scores = jnp.where(causal, _dot_nt(qh, op(k_t[:, ks])), 0.0)
        o = _dot(op(scores), vh) + _dot(qh, op(s_old))
        d_col = jnp.sum(jnp.where(eye, jnp.broadcast_to(d_row[:, ks], (dk, dk)), 0.0), axis=1, keepdims=True)
        s_ref[0, h] = d_col * s_old + _dot_tn(op(k_d[:, ks]), vh)
        o = o * lax.rsqrt(jnp.mean(o * o, axis=-1, keepdims=True) + RMS_EPS) * gain
        r = r_ref[0, :, vs]
        o_ref[0, :, vs] = o * (r * _sigmoid(r))


def _gla(z, la, s0, gain, *, chunk, heads, dk, dv):
    bsz, t, _ = z.shape
    hk, hv = heads * dk, heads * dv
    return pl.pallas_call(
        functools.partial(_gla_kernel, heads=heads, dk=dk, dv=dv),
        grid=(bsz, t // chunk),
        in_specs=[
            pl.BlockSpec((1, chunk, hk), lambda b, c: (b, c, 0)),
            pl.BlockSpec((1, chunk, hk), lambda b, c: (b, c, 1)),
            pl.BlockSpec((1, chunk, hv), lambda b, c: (b, c, 1)),
            pl.BlockSpec((1, chunk, hk), lambda b, c: (b, c, 0)),
            pl.BlockSpec((1, chunk, hv), lambda b, c: (b, c, 2)),
            pl.BlockSpec((1, heads, dk, dv), lambda b, c: (b, 0, 0, 0)),
            pl.BlockSpec((1, dv), lambda b, c: (0, 0)),
        ],
        out_specs=[
            pl.BlockSpec((1, chunk, hv), lambda b, c: (b, c, 0)),
            pl.BlockSpec((1, heads, dk, dv), lambda b, c: (b, 0, 0, 0)),
        ],
        out_shape=[jax.ShapeDtypeStruct((bsz, t, hv), F32),
                   jax.ShapeDtypeStruct((bsz, heads, dk, dv), F32)],
        compiler_params=_params("parallel", "arbitrary"),
        name="gla",
    )(z, z, z, la, z, s0, gain)


def _softmax_update(s, m_ref, l_ref, acc_ref, pv_fn):
    m_old = m_ref[...]
    m_new = jnp.maximum(m_old, jnp.max(s, axis=-1, keepdims=True))
    alpha = jnp.exp2(m_old - m_new)
    width = s.shape[1]
    if width % LANES == 0:
        ps = [jnp.exp2(s[:, k * LANES:(k + 1) * LANES] - m_new) for k in range(width // LANES)]
        psum = functools.reduce(jnp.add, ps)
        p = jnp.concatenate(ps, axis=1) if len(ps) > 1 else ps[0]
    else:
        p = jnp.exp2(s - m_new[:, 0:1])
        lane = lax.broadcasted_iota(jnp.int32, m_old.shape, 1)
        psum = jnp.where(lane == 0, jnp.sum(p, axis=-1, keepdims=True), 0.0)
    l_ref[...] = alpha * l_ref[...] + psum
    reps = acc_ref.shape[1] // LANES
    alpha_w = jnp.concatenate([alpha] * reps, axis=1) if reps > 1 else alpha
    acc_ref[...] = alpha_w * acc_ref[...] + pv_fn(p)
    m_ref[...] = m_new


def _softmax_init(m_ref, l_ref, acc_ref):
    m_ref[...] = jnp.full(m_ref.shape, NEG_BIG, F32)
    l_ref[...] = jnp.zeros(l_ref.shape, F32)
    acc_ref[...] = jnp.zeros(acc_ref.shape, F32)


def _flash_kernel(lam_ref, q_ref, k_ref, v_ref, g_ref, o_ref, m_ref, l_ref, acc_ref, *, tq, wide, dh,
                  out_scale):
    qi = pl.program_id(2)
    _softmax_init(m_ref, l_ref, acc_ref)
    q1 = q_ref[0, :, 0:dh]
    q2 = q_ref[0, :, dh:2 * dh]

    def step(start, width, masked):
        vb = v_ref[0, pl.ds(start, width), :]
        s1 = _dot_nt(q1, k_ref[0, pl.ds(start, width), 0:dh])
        s2 = _dot_nt(q2, k_ref[0, pl.ds(start, width), dh:2 * dh])
        s = jnp.concatenate([s1, s2], axis=0)
        if masked:
            row = lax.broadcasted_iota(jnp.int32, s.shape, 0) % tq
            col = lax.broadcasted_iota(jnp.int32, s.shape, 1)
            s = jnp.where(col <= row, s, NEG_BIG)
        _softmax_update(s, m_ref, l_ref, acc_ref, lambda p: _dot(p.astype(BF16), vb))

    q_start = qi * tq
    n_wide = q_start // wide

    def wide_body(j, carry):
        step(pl.multiple_of(j * wide, wide), wide, False)
        return carry

    lax.fori_loop(0, n_wide, wide_body, 0)

    def narrow_body(j, carry):
        step(pl.multiple_of(n_wide * wide + j * tq, tq), tq, False)
        return carry

    lax.fori_loop(0, (q_start - n_wide * wide) // tq, narrow_body, 0)
    step(pl.multiple_of(q_start, tq), tq, True)

    lam = lam_ref[0]
    l = jnp.sum(l_ref[...], axis=-1, keepdims=True)
    o = acc_ref[0:tq, :] / l[0:tq] - lam * (acc_ref[tq:, :] / l[tq:])
    o = o * lax.rsqrt(jnp.mean(o * o, axis=-1, keepdims=True) + RMS_EPS) * g_ref[...]
    o_ref[0] = o * out_scale


def _flash(lam, q, k, v, gain, *, tq, wide, heads, dh, dv, out_scale):
    bsz, t, _ = q.shape
    return pl.pallas_call(
        functools.partial(_flash_kernel, tq=tq, wide=wide, dh=dh, out_scale=out_scale),
        grid=(bsz, heads, t // tq),
        in_specs=[
            pl.BlockSpec(memory_space=pltpu.SMEM),
            pl.BlockSpec((1, tq, 2 * dh), lambda b, h, i: (b, i, h)),
            pl.BlockSpec((1, t, 2 * dh), lambda b, h, i: (b, 0, h)),
            pl.BlockSpec((1, t, dv), lambda b, h, i: (b, 0, h)),
            pl.BlockSpec((1, dv), lambda b, h, i: (0, 0)),
        ],
        out_specs=pl.BlockSpec((1, tq, dv), lambda b, h, i: (b, i, h)),
        out_shape=jax.ShapeDtypeStruct((bsz, t, heads * dv), F32),
        scratch_shapes=[pltpu.VMEM((2 * tq, LANES), F32), pltpu.VMEM((2 * tq, LANES), F32),
                        pltpu.VMEM((2 * tq, dv), F32)],
        compiler_params=_params("parallel", "parallel", "arbitrary"),
        name="flash_diff_attn",
    )(lam, q, k, v, gain)


def _decode_kernel(pt_ref, lam_ref, q_ref, kn_ref, vn_ref, g_ref, *rest, pages, page, heads, dh, dv, tpad,
                   n_new, out_scale):
    k_refs = rest[:pages]
    v_refs = rest[pages:2 * pages]
    o_ref = rest[2 * pages]
    m_ref, l_ref, acc_ref = rest[2 * pages + 1:]
    c = pl.program_id(1)
    hrows = 2 * tpad

    @pl.when(c == 0)
    def _():
        _softmax_init(m_ref, l_ref, acc_ref)

    q = q_ref[0]
    first = lax.broadcasted_iota(jnp.int32, (hrows, page), 0) < tpad

    def head_scores(h, k1, k2):
        qh = q[h * hrows:(h + 1) * hrows, :]
        return jnp.where(first[:, :k1.shape[0]], _dot_nt(qh, k1), _dot_nt(qh, k2))

    per_tok = heads * 2

    def k_rows(i, h, comp):
        return k_refs[i][0, 0, pl.ds(2 * h + comp, page, stride=per_tok), :].astype(BF16)

    def v_rows(i, h):
        return jnp.concatenate([v_refs[i][0, 0, pl.ds(heads * j + h, page, stride=per_tok), :]
                                for j in range(dv // dh)], axis=1).astype(BF16)

    s = jnp.concatenate(
        [jnp.concatenate([head_scores(h, k_rows(i, h, 0), k_rows(i, h, 1)) for i in range(pages)], axis=1)
         for h in range(heads)], axis=0)

    def pv_pages(p):
        p = p.astype(BF16)
        outs = []
        for h in range(heads):
            ph = p[h * hrows:(h + 1) * hrows, :]
            o = _dot(ph[:, 0:page], v_rows(0, h))
            for i in range(1, pages):
                o = o + _dot(ph[:, i * page:(i + 1) * page], v_rows(i, h))
            outs.append(o)
        return jnp.concatenate(outs, axis=0)

    _softmax_update(s, m_ref, l_ref, acc_ref, pv_pages)

    @pl.when(c == pl.num_programs(1) - 1)
    def _():
        qf = q.astype(F32)
        kn = kn_ref[0]
        vn = vn_ref[0]
        sn = jnp.concatenate(
            [jnp.where(first[:, :tpad],
                       _dot_nt(qf[h * hrows:(h + 1) * hrows, :], kn[:, (2 * h) * dh:(2 * h + 1) * dh]),
                       _dot_nt(qf[h * hrows:(h + 1) * hrows, :], kn[:, (2 * h + 1) * dh:(2 * h + 2) * dh]))
             for h in range(heads)], axis=0)
        tok = lax.broadcasted_iota(jnp.int32, sn.shape, 0) % tpad
        key = lax.broadcasted_iota(jnp.int32, sn.shape, 1)
        sn = jnp.where((key <= tok) & (key < n_new), sn, NEG_BIG)

        def pv_new(p):
            return jnp.concatenate([_dot(p[h * hrows:(h + 1) * hrows, :], vn[:, h * dv:(h + 1) * dv])
                                    for h in range(heads)], axis=0)

        _softmax_update(sn, m_ref, l_ref, acc_ref, pv_new)
        lam = lam_ref[0]
        l = jnp.sum(l_ref[...], axis=-1, keepdims=True)
        a = acc_ref[...] / l
        for h in range(heads):
            oh = a[h * hrows:h * hrows + tpad, :] - lam * a[h * hrows + tpad:(h + 1) * hrows, :]
            oh = oh * lax.rsqrt(jnp.mean(oh * oh, axis=-1, keepdims=True) + RMS_EPS) * g_ref[...]
            o_ref[0, :, h * dv:(h + 1) * dv] = oh * out_scale


def _decode(page_table, lam, q_rows, k_new, v_new, gain, cache_k, cache_v, *, layer, pages, tpad, n_new,
            out_scale):
    dbsz, rows, dh = q_rows.shape
    depth, n_pool, page, heads, dv = cache_v.shape
    d = heads * dv
    n_chunks = page_table.shape[1] // pages
    per_page = page * heads * 2
    k_view = cache_k.reshape(depth, n_pool, per_page, dh)
    v_view = jnp.transpose(cache_v.reshape(depth, n_pool, page, heads, dv // dh, dh), (0, 1, 2, 4, 3, 5))
    v_view = v_view.reshape(depth, n_pool, per_page, dh)

    def page_spec(i):
        return pl.BlockSpec((1, 1, per_page, dh), lambda b, c, pt: (layer, pt[b, c * pages + i], 0, 0))

    return pl.pallas_call(
        functools.partial(_decode_kernel, pages=pages, page=page, heads=heads, dh=dh, dv=dv, tpad=tpad,
                          n_new=n_new, out_scale=out_scale),
        grid_spec=pltpu.PrefetchScalarGridSpec(
            num_scalar_prefetch=1,
            grid=(dbsz, n_chunks),
            in_specs=[
                pl.BlockSpec(memory_space=pltpu.SMEM),
                pl.BlockSpec((1, rows, dh), lambda b, c, pt: (b, 0, 0)),
                pl.BlockSpec((1, tpad, d), lambda b, c, pt: (b, 0, 0)),
                pl.BlockSpec((1, tpad, d), lambda b, c, pt: (b, 0, 0)),
                pl.BlockSpec((1, dv), lambda b, c, pt: (0, 0)),
            ] + [page_spec(i) for i in range(pages)] + [page_spec(i) for i in range(pages)],
            out_specs=pl.BlockSpec((1, tpad, d), lambda b, c, pt: (b, 0, 0)),
            scratch_shapes=[pltpu.VMEM((rows, LANES), F32), pltpu.VMEM((rows, LANES), F32),
                            pltpu.VMEM((rows, dv), F32)],
        ),
        out_shape=jax.ShapeDtypeStruct((dbsz, tpad, d), F32),
        compiler_params=_params("parallel", "arbitrary"),
        name="paged_diff_attn",
    )(page_table, lam, q_rows, k_new, v_new, gain, *([k_view] * pages), *([v_view] * pages))


def _merge_kernel(x_ref, oa_ref, ob_ref, ga_ref, gb_ref, bga_ref, bgb_ref, wa_ref, wb_ref, wo_ref,
                  lg_ref, lb_ref, wr_ref, br_ref, h_ref, comb_ref, *, alpha, n_groups, per_group):
    a = _dot(oa_ref[...].astype(BF16), wa_ref[...])
    b = _dot(ob_ref[...].astype(BF16), wb_ref[...])
    g_a = _sigmoid(ga_ref[...] + bga_ref[...])
    g_b = _sigmoid(gb_ref[...] + bgb_ref[...])
    merged = g_a * a + g_b * b
    u = alpha * x_ref[...] + _dot(merged.astype(BF16), wo_ref[...])
    mu = jnp.mean(u, axis=-1, keepdims=True)
    var = jnp.mean(jnp.square(u - mu), axis=-1, keepdims=True)
    h = (u - mu) * lax.rsqrt(var + LN_EPS) * lg_ref[...] + lb_ref[...]
    h_ref[...] = h

    n_exp = n_groups * per_group
    logits = _dot(h.astype(BF16), wr_ref[...]) + br_ref[...]
    lg = logits[:, 0:n_groups]
    el = logits[:, n_groups:n_groups + n_exp]
    gidx = lax.broadcasted_iota(jnp.int32, lg.shape, 1)
    gmax = jnp.max(lg, axis=-1, keepdims=True)
    g_sel = jnp.min(jnp.where(lg == gmax, gidx, n_groups), axis=-1, keepdims=True)
    g_w = 1.0 / jnp.sum(jnp.exp(lg - gmax), axis=-1, keepdims=True)
    eidx = lax.broadcasted_iota(jnp.int32, el.shape, 1)
    in_group = (eidx >= g_sel * per_group) & (eidx < (g_sel + 1) * per_group)
    e1 = jnp.where(in_group, el, -jnp.inf)
    v1 = jnp.max(e1, axis=-1, keepdims=True)
    i1 = jnp.min(jnp.where(e1 == v1, eidx, n_exp), axis=-1, keepdims=True)
    e2 = jnp.where(eidx == i1, -jnp.inf, e1)
    v2 = jnp.max(e2, axis=-1, keepdims=True)
    i2 = jnp.min(jnp.where(e2 == v2, eidx, n_exp), axis=-1, keepdims=True)
    t = jnp.exp(v2 - v1)
    w1 = g_w / (1.0 + t)
    w2 = g_w * t / (1.0 + t)
    comb_ref[...] = jnp.where(eidx == i1, w1, 0.0) + jnp.where(eidx == i2, w2, 0.0)


def _merge(x, o_gla, o_diff, z, b_ga, b_gb, w_a, w_b, w_o, ln_g, ln_b, w_r, b_r, *, tm, alpha,
           n_groups, per_group, gate_block):
    n, d = x.shape
    n_exp = n_groups * per_group
    row = lambda i: (i, 0)
    fixed = lambda i: (0, 0)
    return pl.pallas_call(
        functools.partial(_merge_kernel, alpha=alpha, n_groups=n_groups, per_group=per_group),
        grid=(n // tm,),
        in_specs=[
            pl.BlockSpec((tm, d), row), pl.BlockSpec((tm, d), row), pl.BlockSpec((tm, d), row),
            pl.BlockSpec((tm, d), lambda i: (i, gate_block)),
            pl.BlockSpec((tm, d), lambda i: (i, gate_block + 1)),
            pl.BlockSpec((1, d), fixed), pl.BlockSpec((1, d), fixed),
            pl.BlockSpec((d, d), fixed), pl.BlockSpec((d, d), fixed), pl.BlockSpec((d, d), fixed),
            pl.BlockSpec((1, d), fixed), pl.BlockSpec((1, d), fixed),
            pl.BlockSpec((d, n_groups + n_exp), fixed), pl.BlockSpec((1, n_groups + n_exp), fixed),
        ],
        out_specs=[pl.BlockSpec((tm, d), row), pl.BlockSpec((tm, n_exp), row)],
        out_shape=[jax.ShapeDtypeStruct((n, d), F32), jax.ShapeDtypeStruct((n, n_exp), F32)],
        compiler_params=_params("parallel"),
        name="merge_ln_route",
    )(x, o_gla, o_diff, z, z, b_ga, b_gb, w_a, w_b, w_o, ln_g, ln_b, w_r, b_r)


def _moe_kernel(h_ref, comb_ref, wg_ref, wu_ref, wd_ref, lg_ref, lb_ref, y_ref, acc_ref, *, alpha):
    e = pl.program_id(1)

    @pl.when(e == 0)
    def _():
        acc_ref[...] = jnp.zeros(acc_ref.shape, F32)

    hb = h_ref[...].astype(BF16)
    gate = _dot(hb, wg_ref[0])
    up = _dot(hb, wu_ref[0])
    hid = gate * _sigmoid(gate) * up
    comb = comb_ref[...]
    eidx = lax.broadcasted_iota(jnp.int32, comb.shape, 1)
    w = jnp.sum(jnp.where(eidx == e, comb, 0.0), axis=-1, keepdims=True)
    acc_ref[...] += w * _dot(hid.astype(BF16), wd_ref[0])

    @pl.when(e == pl.num_programs(1) - 1)
    def _():
        u = alpha * h_ref[...] + acc_ref[...]
        mu = jnp.mean(u, axis=-1, keepdims=True)
        var = jnp.mean(jnp.square(u - mu), axis=-1, keepdims=True)
        y_ref[...] = (u - mu) * lax.rsqrt(var + LN_EPS) * lg_ref[...] + lb_ref[...]


def _moe(h, comb, w_gate, w_up, w_down, ln_g, ln_b, *, tm, alpha):
    n, d = h.shape
    n_exp, _, d_exp = w_gate.shape
    return pl.pallas_call(
        functools.partial(_moe_kernel, alpha=alpha),
        grid=(n // tm, n_exp),
        in_specs=[
            pl.BlockSpec((tm, d), lambda i, e: (i, 0)),
            pl.BlockSpec((tm, n_exp), lambda i, e: (i, 0)),
            pl.BlockSpec((1, d, d_exp), lambda i, e: (e, 0, 0)),
            pl.BlockSpec((1, d, d_exp), lambda i, e: (e, 0, 0)),
            pl.BlockSpec((1, d_exp, d), lambda i, e: (e, 0, 0)),
            pl.BlockSpec((1, d), lambda i, e: (0, 0)),
            pl.BlockSpec((1, d), lambda i, e: (0, 0)),
        ],
        out_specs=pl.BlockSpec((tm, d), lambda i, e: (i, 0)),
        out_shape=jax.ShapeDtypeStruct((n, d), F32),
        scratch_shapes=[pltpu.VMEM((tm, d), F32)],
        compiler_params=_params("parallel", "arbitrary"),
        name="moe_ln",
    )(h, comb, w_gate, w_up, w_down, ln_g, ln_b)


def _rotary_tables(pos, rot_dim, reps):
    half = rot_dim // 2
    inv_freq = ROPE_THETA ** (-jnp.arange(0, rot_dim, 2, dtype=F32) / rot_dim)
    ang = pos[:, None] * inv_freq[None, :]
    cos, sin = jnp.cos(ang), jnp.sin(ang)
    t = pos.shape[0]
    zeros = jnp.zeros((t, LANES - rot_dim), F32)
    zh = jnp.zeros((t, half), F32)
    cos_t = jnp.concatenate([cos, cos, jnp.ones((t, LANES - rot_dim), F32)], axis=1)
    sa_t = jnp.concatenate([-sin, zh, zeros], axis=1)
    sb_t = jnp.concatenate([zh, sin, zeros], axis=1)
    return tuple(jnp.tile(a, (reps, 1)) for a in (cos_t, sa_t, sb_t))


def _pick(n, pref):
    t = min(n, pref)
    while n % t:
        t //= 2
    return t


def kernel(x_prompt, x_sample, cache_k, cache_v, state_gla, page_table, w_in, b_gates, w_gla_g2, b_gla_g2,
           gla_norm_g, w_gla_out, lambda_q1, lambda_k1, lambda_q2, lambda_k2, diff_norm_g, w_diff_out, w_o,
           ln1_g, ln1_b, w_group_router, b_group_router, w_expert_router, b_expert_router, w_e_gate, w_e_up,
           w_e_down, ln2_g, ln2_b):
    bsz, seq, d = x_prompt.shape
    dbsz, dseq, _ = x_sample.shape
    depth = w_in.shape[0]
    _, n_pool, page, heads, _, dh = cache_k.shape
    dv = cache_v.shape[-1]
    _, _, gheads, gdk, gdv = state_gla.shape
    rank = w_gla_g2.shape[1]
    n_groups = w_group_router.shape[-1]
    n_exp = w_expert_router.shape[-1]
    per_group = n_exp // n_groups
    rot_dim = dh // 4
    alpha = (2.0 * depth) ** 0.25
    past_len = page_table.shape[1] * page
    hk = gheads * gdk
    assert hk * 2 == d and gheads * gdv == d and heads * 2 * dh == d and heads * dv == d
    o_glr = 2 * hk + d
    o_gr = o_glr + rank
    o_dq = o_gr + d

    np_tok, ns_tok = bsz * seq, dbsz * dseq
    tpad = SUBLANES
    assert dseq <= tpad
    tm_p = _pick(np_tok, 512)
    tm_s = _pick(ns_tok, 512)
    chunk = _pick(seq, 128)
    tq = _pick(seq, 256)
    wide = _pick(seq, 1024)
    pages = _pick(page_table.shape[1], 8)
    assert seq % tm_p == 0 or tm_p % seq == 0

    pos_p = jnp.arange(seq, dtype=F32)
    pos_s = (past_len + jnp.arange(dseq)).astype(F32)
    tabs_p = _rotary_tables(pos_p, rot_dim, max(1, tm_p // seq))
    tabs_s = _rotary_tables(pos_s, rot_dim, tm_s // dseq)
    q_scale = dh ** -0.5 * LOG2E

    xp = x_prompt.reshape(np_tok, d)
    xs = x_sample.reshape(ns_tok, d)
    outs = [[] for _ in range(6)]
    for l in range(depth):
        lam_init = _lambda_init(l)
        w = w_in[l]
        w_main = jnp.concatenate([w[:, :o_glr], w[:, o_gr:o_dq], w[:, o_dq + 3 * d:], w[:, o_dq:o_dq + 3 * d]],
                                 axis=1).astype(BF16)
        w_glr = w[:, o_glr:o_gr].astype(BF16)
        w_g2 = w_gla_g2[l].astype(BF16)
        b_g2 = b_gla_g2[l].reshape(1, hk)
        b_ga = b_gates[l, :d].reshape(1, d)
        b_gb = b_gates[l, d:].reshape(1, d)
        w_a, w_b, w_out = (t[l].astype(BF16) for t in (w_gla_out, w_diff_out, w_o))
        w_r = jnp.concatenate([w_group_router[l], w_expert_router[l]], axis=1).astype(BF16)
        b_r = jnp.concatenate([b_group_router[l], b_expert_router[l]]).reshape(1, n_groups + n_exp)
        w_eg, w_eu, w_ed = (t[l].astype(BF16) for t in (w_e_gate, w_e_up, w_e_down))
        g_gla = gla_norm_g[l].reshape(1, gdv)
        g_diff = diff_norm_g[l].reshape(1, dv)
        ln1 = (ln1_g[l].reshape(1, d), ln1_b[l].reshape(1, d))
        ln2 = (ln2_g[l].reshape(1, d), ln2_b[l].reshape(1, d))
        lam = (jnp.exp(jnp.sum(lambda_q1[l] * lambda_k1[l])) - jnp.exp(jnp.sum(lambda_q2[l] * lambda_k2[l]))
               + lam_init).astype(F32).reshape(1)

        def head(x, tabs, tm):
            return _in_proj(x, w_main, w_glr, w_g2, b_g2, *tabs, tm=tm, rot_dim=rot_dim, q_scale=q_scale)

        def tail(x, o_gla, o_diff, z, tm):
            h, comb = _merge(x, o_gla, o_diff, z, b_ga, b_gb, w_a, w_b, w_out, *ln1, w_r, b_r, tm=tm,
                             alpha=alpha, n_groups=n_groups, per_group=per_group, gate_block=3)
            return _moe(h, comb, w_eg, w_eu, w_ed, *ln2, tm=_pick(x.shape[0], 1024), alpha=alpha)

        z, la, q_b, k_p, k_b, v_p, v_b = head(xp, tabs_p, tm_p)
        s0 = jnp.zeros((bsz, gheads, gdk, gdv), F32)
        o_gla, s_p = _gla(z.reshape(bsz, seq, -1), la.reshape(bsz, seq, hk), s0, g_gla, chunk=chunk,
                          heads=gheads, dk=gdk, dv=gdv)
        o_diff = _flash(lam, q_b.reshape(bsz, seq, d), k_b.reshape(bsz, seq, d), v_b.reshape(bsz, seq, d),
                        g_diff, tq=tq, wide=wide, heads=heads, dh=dh, dv=dv, out_scale=1.0 - lam_init)
        xp = tail(xp, o_gla.reshape(np_tok, d), o_diff.reshape(np_tok, d), z, tm_p)
        outs[0].append(k_p.reshape(bsz, seq, heads, 2, dh))
        outs[1].append(v_p.reshape(bsz, seq, heads, dv))
        outs[2].append(s_p)

        z, la, q_b, k_s, _, v_s, _ = head(xs, tabs_s, tm_s)
        pad = ((0, 0), (0, tpad - dseq), (0, 0))
        pad3 = lambda a: jnp.pad(a.reshape(dbsz, dseq, -1), pad)
        o_gla, s_s = _gla(pad3(z), pad3(la), state_gla[l], g_gla, chunk=tpad, heads=gheads, dk=gdk, dv=gdv)
        q_rows = jnp.transpose(pad3(q_b).reshape(dbsz, tpad, heads, 2, dh), (0, 2, 3, 1, 4))
        q_rows = q_rows.reshape(dbsz, heads * 2 * tpad, dh)
        o_diff = _decode(page_table, lam, q_rows, pad3(k_s), pad3(v_s), g_diff, cache_k, cache_v, layer=l,
                         pages=pages, tpad=tpad, n_new=dseq, out_scale=1.0 - lam_init)
        xs = tail(xs, o_gla[:, :dseq].reshape(ns_tok, d), o_diff[:, :dseq].reshape(ns_tok, d), z, tm_s)
        outs[3].append(k_s.reshape(dbsz, dseq, heads, 2, dh))
        outs[4].append(v_s.reshape(dbsz, dseq, heads, dv))
        outs[5].append(s_s)

    k_prompt, v_prompt, s_prompt, k_sample, v_sample, s_sample = (jnp.stack(o) for o in outs)
    return (xp.reshape(bsz, seq, d), xs.reshape(dbsz, dseq, d), k_prompt, v_prompt, s_prompt,
            k_sample, v_sample, s_sample)
```

```python
import functools
import math

import jax
import jax.numpy as jnp
from jax import lax
from jax.experimental import pallas as pl
from jax.experimental.pallas import tpu as pltpu

F32 = jnp.float32
BF16 = jnp.bfloat16

GLA_TAU = 16.0
ROPE_THETA = 500000.0
LN_EPS = 1e-5
RMS_EPS = 1e-5
LOG2E = 1.4426950408889634
LANES = 128
SUBLANES = 8
VMEM_LIMIT = 56 * 1024 * 1024
NEG_BIG = -1e30


def _lambda_init(layer):
    return 0.8 - 0.6 * math.exp(-0.3 * layer)


def _sigmoid(x):
    return 1.0 / (1.0 + jnp.exp(-x))


def _dot(a, b):
    return jnp.dot(a, b, preferred_element_type=F32)


def _dot_nt(a, b):
    return lax.dot_general(a, b, (((1,), (1,)), ((), ())), preferred_element_type=F32)


def _dot_tn(a, b):
    return lax.dot_general(a, b, (((0,), (0,)), ((), ())), preferred_element_type=F32)


def _params(*sem):
    return pltpu.CompilerParams(dimension_semantics=sem, vmem_limit_bytes=VMEM_LIMIT)


def _in_proj_kernel(x_ref, w_ref, wglr_ref, wg2_ref, bg2_ref, cos_ref, sa_ref, sb_ref,
                    z_ref, la_ref, qb_ref, k_ref, kb_ref, v_ref, vb_ref, *, n_plain, rot_dim, q_scale, heads):
    j = pl.program_id(1)
    xb = x_ref[...].astype(BF16)
    acc = _dot(xb, w_ref[...])

    def rotary(a):
        half = rot_dim // 2
        cos, sa, sb = cos_ref[...], sa_ref[...], sb_ref[...]
        out = []
        for g in range(a.shape[1] // LANES):
            xg = a[:, g * LANES:(g + 1) * LANES]
            up = pltpu.roll(xg, LANES - half, 1)
            dn = pltpu.roll(xg, half, 1)
            out.append(xg * cos + up * sa + dn * sb)
        return jnp.concatenate(out, axis=1)

    @pl.when(j < n_plain)
    def _():
        z_ref[...] = acc

    @pl.when(j == n_plain)
    def _():
        qb_ref[...] = (rotary(acc) * q_scale).astype(BF16)

    tm = acc.shape[0]
    per_tok = acc.shape[1] // LANES

    @pl.when(j == n_plain + 1)
    def _():
        r = rotary(acc)
        kb_ref[...] = r.astype(BF16)
        for g in range(per_tok):
            k_ref[pl.ds(g, tm, stride=per_tok), :] = r[:, g * LANES:(g + 1) * LANES]

    @pl.when(j == n_plain + 2)
    def _():
        vb_ref[...] = acc.astype(BF16)
        chunks = per_tok // heads
        for g in range(per_tok):
            c, h = g // heads, g % heads
            col = (h * chunks + c) * LANES
            v_ref[pl.ds(g, tm, stride=per_tok), :] = acc[:, col:col + LANES]

    @pl.when(j == 0)
    def _():
        glr = _dot(xb, wglr_ref[...])
        logit = _dot(glr.astype(BF16), wg2_ref[...]) + bg2_ref[...]
        ls = jnp.minimum(logit, 0.0) - jnp.log(1.0 + jnp.exp(-jnp.abs(logit)))
        la_ref[...] = ls * (1.0 / GLA_TAU)


def _in_proj(x, w_main, w_glr, w_g2, b_g2, cos_t, sa_t, sb_t, *, tm, rot_dim, q_scale, heads):
    n, d = x.shape
    n_col = w_main.shape[1] // d
    n_plain = n_col - 3
    tab_blocks = cos_t.shape[0] // tm
    rank = w_glr.shape[1]
    dla = w_g2.shape[1]
    per_tok = d // LANES
    tab_spec = pl.BlockSpec((tm, LANES), lambda i, j: (i % tab_blocks, 0))
    row_spec = pl.BlockSpec((tm, d), lambda i, j: (i, 0))
    stored_spec = pl.BlockSpec((tm * per_tok, LANES), lambda i, j: (i, 0))
    stored_shape = jax.ShapeDtypeStruct((n * per_tok, LANES), F32)
    return pl.pallas_call(
        functools.partial(_in_proj_kernel, n_plain=n_plain, rot_dim=rot_dim, q_scale=q_scale, heads=heads),
        grid=(n // tm, n_col),
        in_specs=[
            row_spec,
            pl.BlockSpec((d, d), lambda i, j: (0, j)),
            pl.BlockSpec((d, rank), lambda i, j: (0, 0)),
            pl.BlockSpec((rank, dla), lambda i, j: (0, 0)),
            pl.BlockSpec((1, dla), lambda i, j: (0, 0)),
            tab_spec, tab_spec, tab_spec,
        ],
        out_specs=[
            pl.BlockSpec((tm, d), lambda i, j: (i, jnp.minimum(j, n_plain - 1))),
            pl.BlockSpec((tm, dla), lambda i, j: (i, 0)),
            row_spec, stored_spec, row_spec, stored_spec, row_spec,
        ],
        out_shape=[jax.ShapeDtypeStruct((n, n_plain * d), F32),
                   jax.ShapeDtypeStruct((n, dla), F32),
                   jax.ShapeDtypeStruct((n, d), BF16),
                   stored_shape, jax.ShapeDtypeStruct((n, d), BF16),
                   stored_shape, jax.ShapeDtypeStruct((n, d), BF16)],
        compiler_params=_params("parallel", "arbitrary"),
        name="in_proj",
    )(x, w_main, w_glr, w_g2, b_g2, cos_t, sa_t, sb_t)


def _gla_kernel(q_ref, k_ref, v_ref, la_ref, r_ref, s0_ref, g_ref, o_ref, s_ref, *, heads, dk, dv):
    c = pl.program_id(1)

    @pl.when(c == 0)
    def _():
        s_ref[...] = s0_ref[...]

    la = la_ref[0]
    ct = la.shape[0]
    row = lax.broadcasted_iota(jnp.int32, (ct, ct), 0)
    col = lax.broadcasted_iota(jnp.int32, (ct, ct), 1)
    causal = col <= row
    tri = causal.astype(F32)
    b = jnp.dot(tri, la, preferred_element_type=F32, precision=lax.Precision.HIGHEST)
    b_last = b[ct - 1:ct, :]
    q_t = q_ref[0] * (dk ** -0.5) * jnp.exp(b)
    k_t = k_ref[0] * jnp.exp(-b)
    k_d = k_ref[0] * jnp.exp(b_last - b)
    d_row = jnp.exp(b_last)
    eye = (lax.broadcasted_iota(jnp.int32, (dk, dk), 0) == lax.broadcasted_iota(jnp.int32, (dk, dk), 1))
    gain = g_ref[...]
    op = (lambda a: a.astype(BF16)) if ct >= 16 else (lambda a: a)
    for h in range(heads):
        ks = slice(h * dk, (h + 1) * dk)
        vs = slice(h * dv, (h + 1) * dv)
        qh = op(q_t[:, ks])
        vh = op(v_ref[0, :, vs])
        s_old = s_ref[0, h]
        scores = jnp.where(causal, _dot_nt(qh, op(k_t[:, ks])), 0.0)
        o = _dot(op(scores), vh) + _dot(qh, op(s_old))
        d_col = jnp.sum(jnp.where(eye, jnp.broadcast_to(d_row[:, ks], (dk, dk)), 0.0), axis=1, keepdims=True)
        s_ref[0, h] = d_col * s_old + _dot_tn(op(k_d[:, ks]), vh)
        o = o * lax.rsqrt(jnp.mean(o * o, axis=-1, keepdims=True) + RMS_EPS) * gain
        r = r_ref[0, :, vs]
        o_ref[0, :, vs] = o * (r * _sigmoid(r))


def _gla(z, la, s0, gain, *, chunk, heads, dk, dv):
    bsz, t, _ = z.shape
    hk, hv = heads * dk, heads * dv
    return pl.pallas_call(
        functools.partial(_gla_kernel, heads=heads, dk=dk, dv=dv),
        grid=(bsz, t // chunk),
        in_specs=[
            pl.BlockSpec((1, chunk, hk), lambda b, c: (b, c, 0)),
            pl.BlockSpec((1, chunk, hk), lambda b, c: (b, c, 1)),
            pl.BlockSpec((1, chunk, hv), lambda b, c: (b, c, 1)),
            pl.BlockSpec((1, chunk, hk), lambda b, c: (b, c, 0)),
            pl.BlockSpec((1, chunk, hv), lambda b, c: (b, c, 2)),
            pl.BlockSpec((1, heads, dk, dv), lambda b, c: (b, 0, 0, 0)),
            pl.BlockSpec((1, dv), lambda b, c: (0, 0)),
        ],
        out_specs=[
            pl.BlockSpec((1, chunk, hv), lambda b, c: (b, c, 0)),
            pl.BlockSpec((1, heads, dk, dv), lambda b, c: (b, 0, 0, 0)),
        ],
        out_shape=[jax.ShapeDtypeStruct((bsz, t, hv), F32),
                   jax.ShapeDtypeStruct((bsz, heads, dk, dv), F32)],
        compiler_params=_params("parallel", "arbitrary"),
        name="gla",
    )(z, z, z, la, z, s0, gain)


def _softmax_update(s, m_ref, l_ref, acc_ref, pv_fn):
    m_old = m_ref[...]
    m_new = jnp.maximum(m_old, jnp.max(s, axis=-1, keepdims=True))
    alpha = jnp.exp2(m_old - m_new)
    width = s.shape[1]
    if width % LANES == 0:
        ps = [jnp.exp2(s[:, k * LANES:(k + 1) * LANES] - m_new) for k in range(width // LANES)]
        psum = functools.reduce(jnp.add, ps)
        p = jnp.concatenate(ps, axis=1) if len(ps) > 1 else ps[0]
    else:
        p = jnp.exp2(s - m_new[:, 0:1])
        lane = lax.broadcasted_iota(jnp.int32, m_old.shape, 1)
        psum = jnp.where(lane == 0, jnp.sum(p, axis=-1, keepdims=True), 0.0)
    l_ref[...] = alpha * l_ref[...] + psum
    reps = acc_ref.shape[1] // LANES
    alpha_w = jnp.concatenate([alpha] * reps, axis=1) if reps > 1 else alpha
    acc_ref[...] = alpha_w * acc_ref[...] + pv_fn(p)
    m_ref[...] = m_new


def _softmax_init(m_ref, l_ref, acc_ref):
    m_ref[...] = jnp.full(m_ref.shape, NEG_BIG, F32)
    l_ref[...] = jnp.zeros(l_ref.shape, F32)
    acc_ref[...] = jnp.zeros(acc_ref.shape, F32)


def _flash_kernel(lam_ref, q_ref, k_ref, v_ref, g_ref, o_ref, m_ref, l_ref, acc_ref, *, tq, wide, dh,
                  out_scale):
    qi = pl.program_id(2)
    _softmax_init(m_ref, l_ref, acc_ref)
    q1 = q_ref[0, :, 0:dh]
    q2 = q_ref[0, :, dh:2 * dh]

    def step(start, width, masked):
        vb = v_ref[0, pl.ds(start, width), :]
        s1 = _dot_nt(q1, k_ref[0, pl.ds(start, width), 0:dh])
        s2 = _dot_nt(q2, k_ref[0, pl.ds(start, width), dh:2 * dh])
        s = jnp.concatenate([s1, s2], axis=0)
        if masked:
            row = lax.broadcasted_iota(jnp.int32, s.shape, 0) % tq
            col = lax.broadcasted_iota(jnp.int32, s.shape, 1)
            s = jnp.where(col <= row, s, NEG_BIG)
        _softmax_update(s, m_ref, l_ref, acc_ref, lambda p: _dot(p.astype(BF16), vb))

    q_start = qi * tq
    n_wide = q_start // wide

    def wide_body(j, carry):
        step(pl.multiple_of(j * wide, wide), wide, False)
        return carry

    lax.fori_loop(0, n_wide, wide_body, 0)

    def narrow_body(j, carry):
        step(pl.multiple_of(n_wide * wide + j * tq, tq), tq, False)
        return carry

    lax.fori_loop(0, (q_start - n_wide * wide) // tq, narrow_body, 0)
    step(pl.multiple_of(q_start, tq), tq, True)

    lam = lam_ref[0]
    l = jnp.sum(l_ref[...], axis=-1, keepdims=True)
    o = acc_ref[0:tq, :] / l[0:tq] - lam * (acc_ref[tq:, :] / l[tq:])
    o = o * lax.rsqrt(jnp.mean(o * o, axis=-1, keepdims=True) + RMS_EPS) * g_ref[...]
    o_ref[0] = o * out_scale


def _flash(lam, q, k, v, gain, *, tq, wide, heads, dh, dv, out_scale):
    bsz, t, _ = q.shape
    return pl.pallas_call(
        functools.partial(_flash_kernel, tq=tq, wide=wide, dh=dh, out_scale=out_scale),
        grid=(bsz, heads, t // tq),
        in_specs=[
            pl.BlockSpec(memory_space=pltpu.SMEM),
            pl.BlockSpec((1, tq, 2 * dh), lambda b, h, i: (b, i, h)),
            pl.BlockSpec((1, t, 2 * dh), lambda b, h, i: (b, 0, h)),
            pl.BlockSpec((1, t, dv), lambda b, h, i: (b, 0, h)),
            pl.BlockSpec((1, dv), lambda b, h, i: (0, 0)),
        ],
        out_specs=pl.BlockSpec((1, tq, dv), lambda b, h, i: (b, i, h)),
        out_shape=jax.ShapeDtypeStruct((bsz, t, heads * dv), F32),
        scratch_shapes=[pltpu.VMEM((2 * tq, LANES), F32), pltpu.VMEM((2 * tq, LANES), F32),
                        pltpu.VMEM((2 * tq, dv), F32)],
        compiler_params=_params("parallel", "parallel", "arbitrary"),
        name="flash_diff_attn",
    )(lam, q, k, v, gain)


def _decode_kernel(pt_ref, lam_ref, q_ref, kn_ref, vn_ref, g_ref, *rest, pages, page, heads, dh, dv, tpad,
                   n_new, out_scale):
    k_refs = rest[:pages]
    v_refs = rest[pages:2 * pages]
    o_ref = rest[2 * pages]
    m_ref, l_ref, acc_ref = rest[2 * pages + 1:]
    c = pl.program_id(1)
    hrows = 2 * tpad

    @pl.when(c == 0)
    def _():
        _softmax_init(m_ref, l_ref, acc_ref)

    q = q_ref[0]
    first = lax.broadcasted_iota(jnp.int32, (hrows, page), 0) < tpad

    def head_scores(h, k1, k2):
        qh = q[h * hrows:(h + 1) * hrows, :]
        return jnp.where(first[:, :k1.shape[0]], _dot_nt(qh, k1), _dot_nt(qh, k2))

    per_tok = heads * 2

    def k_rows(i, h, comp):
        return k_refs[i][0, 0, pl.ds(2 * h + comp, page, stride=per_tok), :].astype(BF16)

    def v_rows(i, h):
        return jnp.concatenate([v_refs[i][0, 0, pl.ds(heads * j + h, page, stride=per_tok), :]
                                for j in range(dv // dh)], axis=1).astype(BF16)

    s = jnp.concatenate(
        [jnp.concatenate([head_scores(h, k_rows(i, h, 0), k_rows(i, h, 1)) for i in range(pages)], axis=1)
         for h in range(heads)], axis=0)

    def pv_pages(p):
        p = p.astype(BF16)
        outs = []
        for h in range(heads):
            ph = p[h * hrows:(h + 1) * hrows, :]
            o = _dot(ph[:, 0:page], v_rows(0, h))
            for i in range(1, pages):
                o = o + _dot(ph[:, i * page:(i + 1) * page], v_rows(i, h))
            outs.append(o)
        return jnp.concatenate(outs, axis=0)

    _softmax_update(s, m_ref, l_ref, acc_ref, pv_pages)

    @pl.when(c == pl.num_programs(1) - 1)
    def _():
        qf = q.astype(F32)

        def kn_rows(h, comp):
            return kn_ref[0, pl.ds(2 * h + comp, tpad, stride=per_tok), :]

        def vn_rows(h):
            return jnp.concatenate([vn_ref[0, pl.ds(heads * j + h, tpad, stride=per_tok), :]
                                    for j in range(dv // dh)], axis=1)

        sn = jnp.concatenate(
            [jnp.where(first[:, :tpad],
                       _dot_nt(qf[h * hrows:(h + 1) * hrows, :], kn_rows(h, 0)),
                       _dot_nt(qf[h * hrows:(h + 1) * hrows, :], kn_rows(h, 1)))
             for h in range(heads)], axis=0)
        tok = lax.broadcasted_iota(jnp.int32, sn.shape, 0) % tpad
        key = lax.broadcasted_iota(jnp.int32, sn.shape, 1)
        sn = jnp.where((key <= tok) & (key < n_new), sn, NEG_BIG)

        def pv_new(p):
            return jnp.concatenate([_dot(p[h * hrows:(h + 1) * hrows, :], vn_rows(h))
                                    for h in range(heads)], axis=0)

        _softmax_update(sn, m_ref, l_ref, acc_ref, pv_new)
        lam = lam_ref[0]
        l = jnp.sum(l_ref[...], axis=-1, keepdims=True)
        a = acc_ref[...] / l
        for h in range(heads):
            oh = a[h * hrows:h * hrows + tpad, :] - lam * a[h * hrows + tpad:(h + 1) * hrows, :]
            oh = oh * lax.rsqrt(jnp.mean(oh * oh, axis=-1, keepdims=True) + RMS_EPS) * g_ref[...]
            o_ref[0, :, h * dv:(h + 1) * dv] = oh * out_scale


def _decode(page_table, lam, q_rows, k_new, v_new, gain, cache_k, cache_v, *, layer, pages, tpad, n_new,
            out_scale):
    dbsz, rows, dh = q_rows.shape
    depth, n_pool, page, heads, dv = cache_v.shape
    d = heads * dv
    n_chunks = page_table.shape[1] // pages
    per_page = page * heads * 2
    k_view = cache_k.reshape(depth, n_pool, per_page, dh)
    v_view = jnp.transpose(cache_v.reshape(depth, n_pool, page, heads, dv // dh, dh), (0, 1, 2, 4, 3, 5))
    v_view = v_view.reshape(depth, n_pool, per_page, dh)

    def page_spec(i):
        return pl.BlockSpec((1, 1, per_page, dh), lambda b, c, pt: (layer, pt[b, c * pages + i], 0, 0))

    return pl.pallas_call(
        functools.partial(_decode_kernel, pages=pages, page=page, heads=heads, dh=dh, dv=dv, tpad=tpad,
                          n_new=n_new, out_scale=out_scale),
        grid_spec=pltpu.PrefetchScalarGridSpec(
            num_scalar_prefetch=1,
            grid=(dbsz, n_chunks),
            in_specs=[
                pl.BlockSpec(memory_space=pltpu.SMEM),
                pl.BlockSpec((1, rows, dh), lambda b, c, pt: (b, 0, 0)),
                pl.BlockSpec((1, tpad * heads * 2, dh), lambda b, c, pt: (b, 0, 0)),
                pl.BlockSpec((1, tpad * heads * 2, dh), lambda b, c, pt: (b, 0, 0)),
                pl.BlockSpec((1, dv), lambda b, c, pt: (0, 0)),
            ] + [page_spec(i) for i in range(pages)] + [page_spec(i) for i in range(pages)],
            out_specs=pl.BlockSpec((1, tpad, d), lambda b, c, pt: (b, 0, 0)),
            scratch_shapes=[pltpu.VMEM((rows, LANES), F32), pltpu.VMEM((rows, LANES), F32),
                            pltpu.VMEM((rows, dv), F32)],
        ),
        out_shape=jax.ShapeDtypeStruct((dbsz, tpad, d), F32),
        compiler_params=_params("parallel", "arbitrary"),
        name="paged_diff_attn",
    )(page_table, lam, q_rows, k_new, v_new, gain, *([k_view] * pages), *([v_view] * pages))


def _merge_kernel(x_ref, oa_ref, ob_ref, ga_ref, gb_ref, bga_ref, bgb_ref, wa_ref, wb_ref, wo_ref,
                  lg_ref, lb_ref, wr_ref, br_ref, h_ref, comb_ref, *, alpha, n_groups, per_group):
    a = _dot(oa_ref[...].astype(BF16), wa_ref[...])
    b = _dot(ob_ref[...].astype(BF16), wb_ref[...])
    g_a = _sigmoid(ga_ref[...] + bga_ref[...])
    g_b = _sigmoid(gb_ref[...] + bgb_ref[...])
    merged = g_a * a + g_b * b
    u = alpha * x_ref[...] + _dot(merged.astype(BF16), wo_ref[...])
    mu = jnp.mean(u, axis=-1, keepdims=True)
    var = jnp.mean(jnp.square(u - mu), axis=-1, keepdims=True)
    h = (u - mu) * lax.rsqrt(var + LN_EPS) * lg_ref[...] + lb_ref[...]
    h_ref[...] = h

    n_exp = n_groups * per_group
    logits = _dot(h.astype(BF16), wr_ref[...]) + br_ref[...]
    lg = logits[:, 0:n_groups]
    el = logits[:, n_groups:n_groups + n_exp]
    gidx = lax.broadcasted_iota(jnp.int32, lg.shape, 1)
    gmax = jnp.max(lg, axis=-1, keepdims=True)
    g_sel = jnp.min(jnp.where(lg == gmax, gidx, n_groups), axis=-1, keepdims=True)
    g_w = 1.0 / jnp.sum(jnp.exp(lg - gmax), axis=-1, keepdims=True)
    eidx = lax.broadcasted_iota(jnp.int32, el.shape, 1)
    in_group = (eidx >= g_sel * per_group) & (eidx < (g_sel + 1) * per_group)
    e1 = jnp.where(in_group, el, -jnp.inf)
    v1 = jnp.max(e1, axis=-1, keepdims=True)
    i1 = jnp.min(jnp.where(e1 == v1, eidx, n_exp), axis=-1, keepdims=True)
    e2 = jnp.where(eidx == i1, -jnp.inf, e1)
    v2 = jnp.max(e2, axis=-1, keepdims=True)
    i2 = jnp.min(jnp.where(e2 == v2, eidx, n_exp), axis=-1, keepdims=True)
    t = jnp.exp(v2 - v1)
    w1 = g_w / (1.0 + t)
    w2 = g_w * t / (1.0 + t)
    comb_ref[...] = jnp.where(eidx == i1, w1, 0.0) + jnp.where(eidx == i2, w2, 0.0)


def _merge(x, o_gla, o_diff, z, b_ga, b_gb, w_a, w_b, w_o, ln_g, ln_b, w_r, b_r, *, tm, alpha,
           n_groups, per_group, gate_block):
    n, d = x.shape
    n_exp = n_groups * per_group
    row = lambda i: (i, 0)
    fixed = lambda i: (0, 0)
    return pl.pallas_call(
        functools.partial(_merge_kernel, alpha=alpha, n_groups=n_groups, per_group=per_group),
        grid=(n // tm,),
        in_specs=[
            pl.BlockSpec((tm, d), row), pl.BlockSpec((tm, d), row), pl.BlockSpec((tm, d), row),
            pl.BlockSpec((tm, d), lambda i: (i, gate_block)),
            pl.BlockSpec((tm, d), lambda i: (i, gate_block + 1)),
            pl.BlockSpec((1, d), fixed), pl.BlockSpec((1, d), fixed),
            pl.BlockSpec((d, d), fixed), pl.BlockSpec((d, d), fixed), pl.BlockSpec((d, d), fixed),
            pl.BlockSpec((1, d), fixed), pl.BlockSpec((1, d), fixed),
            pl.BlockSpec((d, n_groups + n_exp), fixed), pl.BlockSpec((1, n_groups + n_exp), fixed),
        ],
        out_specs=[pl.BlockSpec((tm, d), row), pl.BlockSpec((tm, n_exp), row)],
        out_shape=[jax.ShapeDtypeStruct((n, d), F32), jax.ShapeDtypeStruct((n, n_exp), F32)],
        compiler_params=_params("parallel"),
        name="merge_ln_route",
    )(x, o_gla, o_diff, z, z, b_ga, b_gb, w_a, w_b, w_o, ln_g, ln_b, w_r, b_r)


def _moe_kernel(h_ref, comb_ref, wg_ref, wu_ref, wd_ref, lg_ref, lb_ref, y_ref, acc_ref, hb_ref, rank_ref,
                rank_t_ref, comb_t_ref, *, alpha, cap):
    e = pl.program_id(1)
    tm, n_exp = comb_ref.shape
    exact = lax.Precision.HIGHEST

    @pl.when(e == 0)
    def _():
        acc_ref[...] = jnp.zeros(acc_ref.shape, F32)
        hb_ref[...] = h_ref[...].astype(BF16)
        comb = comb_ref[...]
        sel = (comb > 0.0).astype(BF16)
        before = (lax.broadcasted_iota(jnp.int32, (tm, tm), 1) < lax.broadcasted_iota(jnp.int32, (tm, tm), 0))
        rank = _dot(before.astype(BF16), sel)
        rank_ref[...] = rank
        eye = (lax.broadcasted_iota(jnp.int32, (n_exp, n_exp), 0)
               == lax.broadcasted_iota(jnp.int32, (n_exp, n_exp), 1)).astype(F32)
        nt = (((1,), (1,)), ((), ()))
        rank_t_ref[...] = lax.dot_general(eye, rank, nt, precision=exact, preferred_element_type=F32)
        comb_t_ref[...] = lax.dot_general(eye, comb, nt, precision=exact, preferred_element_type=F32)

    eidx = lax.broadcasted_iota(jnp.int32, (tm, n_exp), 1)
    w_col = jnp.sum(jnp.where(eidx == e, comb_ref[...], 0.0), axis=-1, keepdims=True)
    rank_col = jnp.sum(jnp.where(eidx == e, rank_ref[...], 0.0), axis=-1, keepdims=True)
    w_row = comb_t_ref[pl.ds(e, 1), :]
    rank_row = rank_t_ref[pl.ds(e, 1), :]
    count = jnp.max(jnp.where(w_col > 0.0, rank_col + 1.0, 0.0)).astype(jnp.int32)
    slot_r = lax.broadcasted_iota(jnp.int32, (cap, tm), 0).astype(F32)
    slot_c = lax.broadcasted_iota(jnp.int32, (tm, cap), 1).astype(F32)

    def block(b, carry):
        base = (b * cap).astype(F32)
        pick = ((rank_row - base) == slot_r) & (w_row > 0.0)
        x = _dot(pick.astype(BF16), hb_ref[...]).astype(BF16)
        gate = _dot(x, wg_ref[0])
        up = _dot(x, wu_ref[0])
        hid = gate * _sigmoid(gate) * up
        w_sel = jnp.sum(jnp.where(pick, w_row, 0.0), axis=-1, keepdims=True)
        out = (w_sel * _dot(hid.astype(BF16), wd_ref[0])).astype(BF16)
        place = ((rank_col - base) == slot_c) & (w_col > 0.0)
        acc_ref[...] += _dot(place.astype(BF16), out)
        return carry

    lax.fori_loop(0, (count + cap - 1) // cap, block, 0)

    @pl.when(e == pl.num_programs(1) - 1)
    def _():
        u = alpha * h_ref[...] + acc_ref[...]
        mu = jnp.mean(u, axis=-1, keepdims=True)
        var = jnp.mean(jnp.square(u - mu), axis=-1, keepdims=True)
        y_ref[...] = (u - mu) * lax.rsqrt(var + LN_EPS) * lg_ref[...] + lb_ref[...]


def _moe(h, comb, w_gate, w_up, w_down, ln_g, ln_b, *, tm, alpha, cap):
    n, d = h.shape
    n_exp, _, d_exp = w_gate.shape
    return pl.pallas_call(
        functools.partial(_moe_kernel, alpha=alpha, cap=cap),
        grid=(n // tm, n_exp),
        in_specs=[
            pl.BlockSpec((tm, d), lambda i, e: (i, 0)),
            pl.BlockSpec((tm, n_exp), lambda i, e: (i, 0)),
            pl.BlockSpec((1, d, d_exp), lambda i, e: (e, 0, 0)),
            pl.BlockSpec((1, d, d_exp), lambda i, e: (e, 0, 0)),
            pl.BlockSpec((1, d_exp, d), lambda i, e: (e, 0, 0)),
            pl.BlockSpec((1, d), lambda i, e: (0, 0)),
            pl.BlockSpec((1, d), lambda i, e: (0, 0)),
        ],
        out_specs=pl.BlockSpec((tm, d), lambda i, e: (i, 0)),
        out_shape=jax.ShapeDtypeStruct((n, d), F32),
        scratch_shapes=[pltpu.VMEM((tm, d), F32), pltpu.VMEM((tm, d), BF16), pltpu.VMEM((tm, n_exp), F32),
                        pltpu.VMEM((n_exp, tm), F32), pltpu.VMEM((n_exp, tm), F32)],
        compiler_params=_params("parallel", "arbitrary"),
        name="moe_ln",
    )(h, comb, w_gate, w_up, w_down, ln_g, ln_b)


def _rotary_tables(pos, rot_dim, reps):
    half = rot_dim // 2
    inv_freq = ROPE_THETA ** (-jnp.arange(0, rot_dim, 2, dtype=F32) / rot_dim)
    ang = pos[:, None] * inv_freq[None, :]
    cos, sin = jnp.cos(ang), jnp.sin(ang)
    t = pos.shape[0]
    zeros = jnp.zeros((t, LANES - rot_dim), F32)
    zh = jnp.zeros((t, half), F32)
    cos_t = jnp.concatenate([cos, cos, jnp.ones((t, LANES - rot_dim), F32)], axis=1)
    sa_t = jnp.concatenate([-sin, zh, zeros], axis=1)
    sb_t = jnp.concatenate([zh, sin, zeros], axis=1)
    return tuple(jnp.tile(a, (reps, 1)) for a in (cos_t, sa_t, sb_t))


def _pick(n, pref):
    t = min(n, pref)
    while n % t:
        t //= 2
    return t


def kernel(x_prompt, x_sample, cache_k, cache_v, state_gla, page_table, w_in, b_gates, w_gla_g2, b_gla_g2,
           gla_norm_g, w_gla_out, lambda_q1, lambda_k1, lambda_q2, lambda_k2, diff_norm_g, w_diff_out, w_o,
           ln1_g, ln1_b, w_group_router, b_group_router, w_expert_router, b_expert_router, w_e_gate, w_e_up,
           w_e_down, ln2_g, ln2_b):
    bsz, seq, d = x_prompt.shape
    dbsz, dseq, _ = x_sample.shape
    depth = w_in.shape[0]
    _, n_pool, page, heads, _, dh = cache_k.shape
    dv = cache_v.shape[-1]
    _, _, gheads, gdk, gdv = state_gla.shape
    rank = w_gla_g2.shape[1]
    n_groups = w_group_router.shape[-1]
    n_exp = w_expert_router.shape[-1]
    per_group = n_exp // n_groups
    rot_dim = dh // 4
    alpha = (2.0 * depth) ** 0.25
    past_len = page_table.shape[1] * page
    hk = gheads * gdk
    assert hk * 2 == d and gheads * gdv == d and heads * 2 * dh == d and heads * dv == d
    o_glr = 2 * hk + d
    o_gr = o_glr + rank
    o_dq = o_gr + d

    np_tok, ns_tok = bsz * seq, dbsz * dseq
    tpad = SUBLANES
    assert dseq <= tpad
    tm_p = _pick(np_tok, 512)
    tm_s = _pick(ns_tok, 512)
    chunk = _pick(seq, 128)
    tq = _pick(seq, 512)
    wide = _pick(seq, 1024)
    pages = _pick(page_table.shape[1], 16)
    assert seq % tm_p == 0 or tm_p % seq == 0

    pos_p = jnp.arange(seq, dtype=F32)
    pos_s = (past_len + jnp.arange(dseq)).astype(F32)
    tabs_p = _rotary_tables(pos_p, rot_dim, max(1, tm_p // seq))
    tabs_s = _rotary_tables(pos_s, rot_dim, tm_s // dseq)
    q_scale = dh ** -0.5 * LOG2E

    xp = x_prompt.reshape(np_tok, d)
    xs = x_sample.reshape(ns_tok, d)
    outs = [[] for _ in range(6)]
    for l in range(depth):
        lam_init = _lambda_init(l)
        w = w_in[l]
        w_main = jnp.concatenate([w[:, :o_glr], w[:, o_gr:o_dq], w[:, o_dq + 3 * d:], w[:, o_dq:o_dq + 3 * d]],
                                 axis=1).astype(BF16)
        w_glr = w[:, o_glr:o_gr].astype(BF16)
        w_g2 = w_gla_g2[l].astype(BF16)
        b_g2 = b_gla_g2[l].reshape(1, hk)
        b_ga = b_gates[l, :d].reshape(1, d)
        b_gb = b_gates[l, d:].reshape(1, d)
        w_a, w_b, w_out = (t[l].astype(BF16) for t in (w_gla_out, w_diff_out, w_o))
        w_r = jnp.concatenate([w_group_router[l], w_expert_router[l]], axis=1).astype(BF16)
        b_r = jnp.concatenate([b_group_router[l], b_expert_router[l]]).reshape(1, n_groups + n_exp)
        w_eg, w_eu, w_ed = (t[l].astype(BF16) for t in (w_e_gate, w_e_up, w_e_down))
        g_gla = gla_norm_g[l].reshape(1, gdv)
        g_diff = diff_norm_g[l].reshape(1, dv)
        ln1 = (ln1_g[l].reshape(1, d), ln1_b[l].reshape(1, d))
        ln2 = (ln2_g[l].reshape(1, d), ln2_b[l].reshape(1, d))
        lam = (jnp.exp(jnp.sum(lambda_q1[l] * lambda_k1[l])) - jnp.exp(jnp.sum(lambda_q2[l] * lambda_k2[l]))
               + lam_init).astype(F32).reshape(1)

        def head(x, tabs, tm):
            return _in_proj(x, w_main, w_glr, w_g2, b_g2, *tabs, tm=tm, rot_dim=rot_dim, q_scale=q_scale,
                            heads=heads)

        def v_logical(v_stored, lead):
            v5 = v_stored.reshape(*lead, dv // dh, heads, dh)
            return jnp.swapaxes(v5, -3, -2).reshape(*lead, heads, dv)

        def tail(x, o_gla, o_diff, z, tm):
            h, comb = _merge(x, o_gla, o_diff, z, b_ga, b_gb, w_a, w_b, w_out, *ln1, w_r, b_r, tm=tm,
                             alpha=alpha, n_groups=n_groups, per_group=per_group, gate_block=3)
            return _moe(h, comb, w_eg, w_eu, w_ed, *ln2, tm=_pick(x.shape[0], 1024), alpha=alpha, cap=160)

        z, la, q_b, k_p, k_b, v_p, v_b = head(xp, tabs_p, tm_p)
        s0 = jnp.zeros((bsz, gheads, gdk, gdv), F32)
        o_gla, s_p = _gla(z.reshape(bsz, seq, -1), la.reshape(bsz, seq, hk), s0, g_gla, chunk=chunk,
                          heads=gheads, dk=gdk, dv=gdv)
        o_diff = _flash(lam, q_b.reshape(bsz, seq, d), k_b.reshape(bsz, seq, d), v_b.reshape(bsz, seq, d),
                        g_diff, tq=tq, wide=wide, heads=heads, dh=dh, dv=dv, out_scale=1.0 - lam_init)
        xp = tail(xp, o_gla.reshape(np_tok, d), o_diff.reshape(np_tok, d), z, tm_p)
        outs[0].append(k_p.reshape(bsz, seq, heads, 2, dh))
        outs[1].append(v_logical(v_p, (bsz, seq)))
        outs[2].append(s_p)

        z, la, q_b, k_s, _, v_s, _ = head(xs, tabs_s, tm_s)
        pad = ((0, 0), (0, tpad - dseq), (0, 0))
        pad3 = lambda a: jnp.pad(a.reshape(dbsz, dseq, -1), pad)
        o_gla, s_s = _gla(pad3(z), pad3(la), state_gla[l], g_gla, chunk=tpad, heads=gheads, dk=gdk, dv=gdv)
        q_rows = jnp.transpose(pad3(q_b).reshape(dbsz, tpad, heads, 2, dh), (0, 2, 3, 1, 4))
        q_rows = q_rows.reshape(dbsz, heads * 2 * tpad, dh)
        per_tok = d // dh
        pad_rows = lambda a: jnp.pad(a.reshape(dbsz, dseq * per_tok, dh),
                                     ((0, 0), (0, (tpad - dseq) * per_tok), (0, 0)))
        o_diff = _decode(page_table, lam, q_rows, pad_rows(k_s), pad_rows(v_s), g_diff, cache_k, cache_v, layer=l,
                         pages=pages, tpad=tpad, n_new=dseq, out_scale=1.0 - lam_init)
        xs = tail(xs, o_gla[:, :dseq].reshape(ns_tok, d), o_diff[:, :dseq].reshape(ns_tok, d), z, tm_s)
        outs[3].append(k_s.reshape(dbsz, dseq, heads, 2, dh))
        outs[4].append(v_logical(v_s, (dbsz, dseq)))
        outs[5].append(s_s)

    k_prompt, v_prompt, s_prompt, k_sample, v_sample, s_sample = (jnp.stack(o) for o in outs)
    return (xp.reshape(bsz, seq, d), xs.reshape(dbsz, dseq, d), k_prompt, v_prompt, s_prompt,
            k_sample, v_sample, s_sample)
```

```python
import functools
import math

import jax
import jax.numpy as jnp
from jax import lax
from jax.experimental import pallas as pl
from jax.experimental.pallas import tpu as pltpu

F32 = jnp.float32
BF16 = jnp.bfloat16

GLA_TAU = 16.0
ROPE_THETA = 500000.0
LN_EPS = 1e-5
RMS_EPS = 1e-5
LOG2E = 1.4426950408889634
LANES = 128
SUBLANES = 8
VMEM_LIMIT = 56 * 1024 * 1024
NEG_BIG = -1e30


def _lambda_init(layer):
    return 0.8 - 0.6 * math.exp(-0.3 * layer)


def _sigmoid(x):
    return 1.0 / (1.0 + jnp.exp(-x))


def _dot(a, b):
    return jnp.dot(a, b, preferred_element_type=F32)


def _dot_nt(a, b):
    return lax.dot_general(a, b, (((1,), (1,)), ((), ())), preferred_element_type=F32)


def _dot_tn(a, b):
    return lax.dot_general(a, b, (((0,), (0,)), ((), ())), preferred_element_type=F32)


def _params(*sem):
    return pltpu.CompilerParams(dimension_semantics=sem, vmem_limit_bytes=VMEM_LIMIT)


def _in_proj_kernel(x_ref, w_ref, wglr_ref, wg2_ref, bg2_ref, cos_ref, sa_ref, sb_ref,
                    z_ref, la_ref, qb_ref, k_ref, kb_ref, v_ref, vb_ref, *, n_plain, rot_dim, q_scale, heads):
    j = pl.program_id(1)
    xb = x_ref[...].astype(BF16)
    acc = _dot(xb, w_ref[...])

    def rotary(a):
        half = rot_dim // 2
        cos, sa, sb = cos_ref[...], sa_ref[...], sb_ref[...]
        out = []
        for g in range(a.shape[1] // LANES):
            xg = a[:, g * LANES:(g + 1) * LANES]
            up = pltpu.roll(xg, LANES - half, 1)
            dn = pltpu.roll(xg, half, 1)
            out.append(xg * cos + up * sa + dn * sb)
        return jnp.concatenate(out, axis=1)

    @pl.when(j < n_plain)
    def _():
        z_ref[...] = acc

    @pl.when(j == n_plain)
    def _():
        qb_ref[...] = (rotary(acc) * q_scale).astype(BF16)

    tm = acc.shape[0]
    per_tok = acc.shape[1] // LANES

    @pl.when(j == n_plain + 1)
    def _():
        r = rotary(acc)
        kb_ref[...] = r.astype(BF16)
        for g in range(per_tok):
            k_ref[pl.ds(g, tm, stride=per_tok), :] = r[:, g * LANES:(g + 1) * LANES]

    @pl.when(j == n_plain + 2)
    def _():
        vb_ref[...] = acc.astype(BF16)
        chunks = per_tok // heads
        for g in range(per_tok):
            c, h = g // heads, g % heads
            col = (h * chunks + c) * LANES
            v_ref[pl.ds(g, tm, stride=per_tok), :] = acc[:, col:col + LANES]

    @pl.when(j == 0)
    def _():
        glr = _dot(xb, wglr_ref[...])
        logit = _dot(glr.astype(BF16), wg2_ref[...]) + bg2_ref[...]
        ls = jnp.minimum(logit, 0.0) - jnp.log(1.0 + jnp.exp(-jnp.abs(logit)))
        la_ref[...] = ls * (1.0 / GLA_TAU)


def _in_proj(x, w_main, w_glr, w_g2, b_g2, cos_t, sa_t, sb_t, *, tm, rot_dim, q_scale, heads):
    n, d = x.shape
    n_col = w_main.shape[1] // d
    n_plain = n_col - 3
    tab_blocks = cos_t.shape[0] // tm
    rank = w_glr.shape[1]
    dla = w_g2.shape[1]
    per_tok = d // LANES
    tab_spec = pl.BlockSpec((tm, LANES), lambda i, j: (i % tab_blocks, 0))
    row_spec = pl.BlockSpec((tm, d), lambda i, j: (i, 0))
    stored_shape = jax.ShapeDtypeStruct((n * per_tok, LANES), F32)

    def once(shape):
        return pl.BlockSpec(shape, lambda i, j: (i, 0), pipeline_mode=pl.Buffered(1))

    return pl.pallas_call(
        functools.partial(_in_proj_kernel, n_plain=n_plain, rot_dim=rot_dim, q_scale=q_scale, heads=heads),
        grid=(n // tm, n_col),
        in_specs=[
            row_spec,
            pl.BlockSpec((d, d), lambda i, j: (0, j)),
            pl.BlockSpec((d, rank), lambda i, j: (0, 0)),
            pl.BlockSpec((rank, dla), lambda i, j: (0, 0)),
            pl.BlockSpec((1, dla), lambda i, j: (0, 0)),
            tab_spec, tab_spec, tab_spec,
        ],
        out_specs=[
            pl.BlockSpec((tm, d), lambda i, j: (i, jnp.minimum(j, n_plain - 1))),
            once((tm, dla)), once((tm, d)), once((tm * per_tok, LANES)), once((tm, d)),
            once((tm * per_tok, LANES)), once((tm, d)),
        ],
        out_shape=[jax.ShapeDtypeStruct((n, n_plain * d), F32),
                   jax.ShapeDtypeStruct((n, dla), F32),
                   jax.ShapeDtypeStruct((n, d), BF16),
                   stored_shape, jax.ShapeDtypeStruct((n, d), BF16),
                   stored_shape, jax.ShapeDtypeStruct((n, d), BF16)],
        compiler_params=_params("parallel", "arbitrary"),
        name="in_proj",
    )(x, w_main, w_glr, w_g2, b_g2, cos_t, sa_t, sb_t)


def _gla_kernel(q_ref, k_ref, v_ref, la_ref, r_ref, s0_ref, g_ref, o_ref, s_ref, *, heads, dk, dv, chunk):
    c = pl.program_id(1)

    @pl.when(c == 0)
    def _():
        s_ref[...] = s0_ref[...]

    ct = chunk
    row = lax.broadcasted_iota(jnp.int32, (ct, ct), 0)
    col = lax.broadcasted_iota(jnp.int32, (ct, ct), 1)
    causal = col <= row
    tri = causal.astype(F32)
    eye = (lax.broadcasted_iota(jnp.int32, (dk, dk), 0) == lax.broadcasted_iota(jnp.int32, (dk, dk), 1))
    gain = g_ref[...]
    op = (lambda a: a.astype(BF16)) if ct >= 16 else (lambda a: a)

    def one_chunk(rows):
        la = la_ref[0, rows, :]
        b = jnp.dot(tri, la, preferred_element_type=F32, precision=lax.Precision.HIGHEST)
        b_last = b[ct - 1:ct, :]
        kk = k_ref[0, rows, :]
        q_t = q_ref[0, rows, :] * (dk ** -0.5) * jnp.exp(b)
        k_t = kk * jnp.exp(-b)
        k_d = kk * jnp.exp(b_last - b)
        d_row = jnp.exp(b_last)
        for h in range(heads):
            ks = slice(h * dk, (h + 1) * dk)
            vs = slice(h * dv, (h + 1) * dv)
            qh = op(q_t[:, ks])
            vh = op(v_ref[0, rows, vs])
            s_old = s_ref[0, h]
            scores = jnp.where(causal, _dot_nt(qh, op(k_t[:, ks])), 0.0)
            o = _dot(op(scores), vh) + _dot(qh, op(s_old))
            d_col = jnp.sum(jnp.where(eye, jnp.broadcast_to(d_row[:, ks], (dk, dk)), 0.0), axis=1,
                            keepdims=True)
            s_ref[0, h] = d_col * s_old + _dot_tn(op(k_d[:, ks]), vh)
            o = o * lax.rsqrt(jnp.mean(o * o, axis=-1, keepdims=True) + RMS_EPS) * gain
            r = r_ref[0, rows, vs]
            o_ref[0, rows, vs] = o * (r * _sigmoid(r))

    for sub in range(la_ref.shape[1] // chunk):
        one_chunk(slice(sub * chunk, (sub + 1) * chunk))


def _gla(z, la, s0, gain, *, chunk, per_step, heads, dk, dv):
    bsz, t, _ = z.shape
    hk, hv = heads * dk, heads * dv
    blk = chunk * per_step
    return pl.pallas_call(
        functools.partial(_gla_kernel, heads=heads, dk=dk, dv=dv, chunk=chunk),
        grid=(bsz, t // blk),
        in_specs=[
            pl.BlockSpec((1, blk, hk), lambda b, c: (b, c, 0)),
            pl.BlockSpec((1, blk, hk), lambda b, c: (b, c, 1)),
            pl.BlockSpec((1, blk, hv), lambda b, c: (b, c, 1)),
            pl.BlockSpec((1, blk, hk), lambda b, c: (b, c, 0)),
            pl.BlockSpec((1, blk, hv), lambda b, c: (b, c, 2)),
            pl.BlockSpec((1, heads, dk, dv), lambda b, c: (b, 0, 0, 0)),
            pl.BlockSpec((1, dv), lambda b, c: (0, 0)),
        ],
        out_specs=[
            pl.BlockSpec((1, blk, hv), lambda b, c: (b, c, 0)),
            pl.BlockSpec((1, heads, dk, dv), lambda b, c: (b, 0, 0, 0)),
        ],
        out_shape=[jax.ShapeDtypeStruct((bsz, t, hv), F32),
                   jax.ShapeDtypeStruct((bsz, heads, dk, dv), F32)],
        compiler_params=_params("parallel", "arbitrary"),
        name="gla",
    )(z, z, z, la, z, s0, gain)


def _softmax_update(s, m_ref, l_ref, acc_ref, pv_fn):
    m_old = m_ref[...]
    m_new = jnp.maximum(m_old, jnp.max(s, axis=-1, keepdims=True))
    alpha = jnp.exp2(m_old - m_new)
    width = s.shape[1]
    if width % LANES == 0:
        ps = [jnp.exp2(s[:, k * LANES:(k + 1) * LANES] - m_new) for k in range(width // LANES)]
        psum = functools.reduce(jnp.add, ps)
        p = jnp.concatenate(ps, axis=1) if len(ps) > 1 else ps[0]
    else:
        p = jnp.exp2(s - m_new[:, 0:1])
        lane = lax.broadcasted_iota(jnp.int32, m_old.shape, 1)
        psum = jnp.where(lane == 0, jnp.sum(p, axis=-1, keepdims=True), 0.0)
    l_ref[...] = alpha * l_ref[...] + psum
    reps = acc_ref.shape[1] // LANES
    alpha_w = jnp.concatenate([alpha] * reps, axis=1) if reps > 1 else alpha
    acc_ref[...] = alpha_w * acc_ref[...] + pv_fn(p)
    m_ref[...] = m_new


def _softmax_init(m_ref, l_ref, acc_ref):
    m_ref[...] = jnp.full(m_ref.shape, NEG_BIG, F32)
    l_ref[...] = jnp.zeros(l_ref.shape, F32)
    acc_ref[...] = jnp.zeros(acc_ref.shape, F32)


def _flash_kernel(lam_ref, q_ref, k_ref, v_ref, g_ref, o_ref, m_ref, l_ref, acc_ref, *, tq, wide, dh,
                  out_scale):
    qi = pl.program_id(2)
    _softmax_init(m_ref, l_ref, acc_ref)
    q1 = q_ref[0, :, 0:dh]
    q2 = q_ref[0, :, dh:2 * dh]

    def step(start, width, masked):
        vb = v_ref[0, pl.ds(start, width), :]
        s1 = _dot_nt(q1, k_ref[0, pl.ds(start, width), 0:dh])
        s2 = _dot_nt(q2, k_ref[0, pl.ds(start, width), dh:2 * dh])
        s = jnp.concatenate([s1, s2], axis=0)
        if masked:
            row = lax.broadcasted_iota(jnp.int32, s.shape, 0) % tq
            col = lax.broadcasted_iota(jnp.int32, s.shape, 1)
            s = jnp.where(col <= row, s, NEG_BIG)
        _softmax_update(s, m_ref, l_ref, acc_ref, lambda p: _dot(p.astype(BF16), vb))

    q_start = qi * tq
    n_wide = q_start // wide

    def wide_body(j, carry):
        step(pl.multiple_of(j * wide, wide), wide, False)
        return carry

    lax.fori_loop(0, n_wide, wide_body, 0)

    def narrow_body(j, carry):
        step(pl.multiple_of(n_wide * wide + j * tq, tq), tq, False)
        return carry

    lax.fori_loop(0, (q_start - n_wide * wide) // tq, narrow_body, 0)
    step(pl.multiple_of(q_start, tq), tq, True)

    lam = lam_ref[0]
    l = jnp.sum(l_ref[...], axis=-1, keepdims=True)
    o = acc_ref[0:tq, :] / l[0:tq] - lam * (acc_ref[tq:, :] / l[tq:])
    o = o * lax.rsqrt(jnp.mean(o * o, axis=-1, keepdims=True) + RMS_EPS) * g_ref[...]
    o_ref[0] = o * out_scale


def _flash(lam, q, k, v, gain, *, tq, wide, heads, dh, dv, out_scale):
    bsz, t, _ = q.shape
    return pl.pallas_call(
        functools.partial(_flash_kernel, tq=tq, wide=wide, dh=dh, out_scale=out_scale),
        grid=(bsz, heads, t // tq),
        in_specs=[
            pl.BlockSpec(memory_space=pltpu.SMEM),
            pl.BlockSpec((1, tq, 2 * dh), lambda b, h, i: (b, i, h)),
            pl.BlockSpec((1, t, 2 * dh), lambda b, h, i: (b, 0, h)),
            pl.BlockSpec((1, t, dv), lambda b, h, i: (b, 0, h)),
            pl.BlockSpec((1, dv), lambda b, h, i: (0, 0)),
        ],
        out_specs=pl.BlockSpec((1, tq, dv), lambda b, h, i: (b, i, h)),
        out_shape=jax.ShapeDtypeStruct((bsz, t, heads * dv), F32),
        scratch_shapes=[pltpu.VMEM((2 * tq, LANES), F32), pltpu.VMEM((2 * tq, LANES), F32),
                        pltpu.VMEM((2 * tq, dv), F32)],
        compiler_params=_params("parallel", "parallel", "arbitrary"),
        name="flash_diff_attn",
    )(lam, q, k, v, gain)


def _decode_kernel(pt_ref, lam_ref, q_ref, kn_ref, vn_ref, g_ref, *rest, pages, page, heads, dh, dv, tpad,
                   n_new, out_scale):
    k_refs = rest[:pages]
    v_refs = rest[pages:2 * pages]
    o_ref = rest[2 * pages]
    m_ref, l_ref, acc_ref = rest[2 * pages + 1:]
    c = pl.program_id(1)
    hrows = 2 * tpad

    @pl.when(c == 0)
    def _():
        _softmax_init(m_ref, l_ref, acc_ref)

    q = q_ref[0]
    first = lax.broadcasted_iota(jnp.int32, (hrows, page), 0) < tpad

    def head_scores(h, k1, k2):
        qh = q[h * hrows:(h + 1) * hrows, :]
        return jnp.where(first[:, :k1.shape[0]], _dot_nt(qh, k1), _dot_nt(qh, k2))

    per_tok = heads * 2

    def k_rows(i, h, comp):
        return k_refs[i][0, 0, pl.ds(2 * h + comp, page, stride=per_tok), :].astype(BF16)

    def v_rows(i, h):
        return jnp.concatenate([v_refs[i][0, 0, pl.ds(heads * j + h, page, stride=per_tok), :]
                                for j in range(dv // dh)], axis=1).astype(BF16)

    s = jnp.concatenate(
        [jnp.concatenate([head_scores(h, k_rows(i, h, 0), k_rows(i, h, 1)) for i in range(pages)], axis=1)
         for h in range(heads)], axis=0)

    def pv_pages(p):
        p = p.astype(BF16)
        outs = []
        for h in range(heads):
            ph = p[h * hrows:(h + 1) * hrows, :]
            o = _dot(ph[:, 0:page], v_rows(0, h))
            for i in range(1, pages):
                o = o + _dot(ph[:, i * page:(i + 1) * page], v_rows(i, h))
            outs.append(o)
        return jnp.concatenate(outs, axis=0)

    _softmax_update(s, m_ref, l_ref, acc_ref, pv_pages)

    @pl.when(c == pl.num_programs(1) - 1)
    def _():
        qf = q.astype(F32)

        def kn_rows(h, comp):
            return kn_ref[0, pl.ds(2 * h + comp, tpad, stride=per_tok), :]

        def vn_rows(h):
            return jnp.concatenate([vn_ref[0, pl.ds(heads * j + h, tpad, stride=per_tok), :]
                                    for j in range(dv // dh)], axis=1)

        sn = jnp.concatenate(
            [jnp.where(first[:, :tpad],
                       _dot_nt(qf[h * hrows:(h + 1) * hrows, :], kn_rows(h, 0)),
                       _dot_nt(qf[h * hrows:(h + 1) * hrows, :], kn_rows(h, 1)))
             for h in range(heads)], axis=0)
        tok = lax.broadcasted_iota(jnp.int32, sn.shape, 0) % tpad
        key = lax.broadcasted_iota(jnp.int32, sn.shape, 1)
        sn = jnp.where((key <= tok) & (key < n_new), sn, NEG_BIG)

        def pv_new(p):
            return jnp.concatenate([_dot(p[h * hrows:(h + 1) * hrows, :], vn_rows(h))
                                    for h in range(heads)], axis=0)

        _softmax_update(sn, m_ref, l_ref, acc_ref, pv_new)
        lam = lam_ref[0]
        l = jnp.sum(l_ref[...], axis=-1, keepdims=True)
        a = acc_ref[...] / l
        for h in range(heads):
            oh = a[h * hrows:h * hrows + tpad, :] - lam * a[h * hrows + tpad:(h + 1) * hrows, :]
            oh = oh * lax.rsqrt(jnp.mean(oh * oh, axis=-1, keepdims=True) + RMS_EPS) * g_ref[...]
            o_ref[0, :, h * dv:(h + 1) * dv] = oh * out_scale


def _decode(page_table, lam, q_rows, k_new, v_new, gain, cache_k, cache_v, *, layer, pages, tpad, n_new,
            out_scale):
    dbsz, rows, dh = q_rows.shape
    depth, n_pool, page, heads, dv = cache_v.shape
    d = heads * dv
    n_chunks = page_table.shape[1] // pages
    per_page = page * heads * 2
    k_view = cache_k.reshape(depth, n_pool, per_page, dh)
    v_view = jnp.transpose(cache_v.reshape(depth, n_pool, page, heads, dv // dh, dh), (0, 1, 2, 4, 3, 5))
    v_view = v_view.reshape(depth, n_pool, per_page, dh)

    def page_spec(i):
        return pl.BlockSpec((1, 1, per_page, dh), lambda b, c, pt: (layer, pt[b, c * pages + i], 0, 0))

    return pl.pallas_call(
        functools.partial(_decode_kernel, pages=pages, page=page, heads=heads, dh=dh, dv=dv, tpad=tpad,
                          n_new=n_new, out_scale=out_scale),
        grid_spec=pltpu.PrefetchScalarGridSpec(
            num_scalar_prefetch=1,
            grid=(dbsz, n_chunks),
            in_specs=[
                pl.BlockSpec(memory_space=pltpu.SMEM),
                pl.BlockSpec((1, rows, dh), lambda b, c, pt: (b, 0, 0)),
                pl.BlockSpec((1, tpad * heads * 2, dh), lambda b, c, pt: (b, 0, 0)),
                pl.BlockSpec((1, tpad * heads * 2, dh), lambda b, c, pt: (b, 0, 0)),
                pl.BlockSpec((1, dv), lambda b, c, pt: (0, 0)),
            ] + [page_spec(i) for i in range(pages)] + [page_spec(i) for i in range(pages)],
            out_specs=pl.BlockSpec((1, tpad, d), lambda b, c, pt: (b, 0, 0)),
            scratch_shapes=[pltpu.VMEM((rows, LANES), F32), pltpu.VMEM((rows, LANES), F32),
                            pltpu.VMEM((rows, dv), F32)],
        ),
        out_shape=jax.ShapeDtypeStruct((dbsz, tpad, d), F32),
        compiler_params=_params("parallel", "arbitrary"),
        name="paged_diff_attn",
    )(page_table, lam, q_rows, k_new, v_new, gain, *([k_view] * pages), *([v_view] * pages))


def _merge_kernel(x_ref, oa_ref, ob_ref, ga_ref, gb_ref, bga_ref, bgb_ref, wa_ref, wb_ref, wo_ref,
                  lg_ref, lb_ref, wr_ref, br_ref, h_ref, comb_ref, *, alpha, n_groups, per_group):
    a = _dot(oa_ref[...].astype(BF16), wa_ref[...])
    b = _dot(ob_ref[...].astype(BF16), wb_ref[...])
    g_a = _sigmoid(ga_ref[...] + bga_ref[...])
    g_b = _sigmoid(gb_ref[...] + bgb_ref[...])
    merged = g_a * a + g_b * b
    u = alpha * x_ref[...] + _dot(merged.astype(BF16), wo_ref[...])
    mu = jnp.mean(u, axis=-1, keepdims=True)
    var = jnp.mean(jnp.square(u - mu), axis=-1, keepdims=True)
    h = (u - mu) * lax.rsqrt(var + LN_EPS) * lg_ref[...] + lb_ref[...]
    h_ref[...] = h

    n_exp = n_groups * per_group
    logits = _dot(h.astype(BF16), wr_ref[...]) + br_ref[...]
    lg = logits[:, 0:n_groups]
    el = logits[:, n_groups:n_groups + n_exp]
    gidx = lax.broadcasted_iota(jnp.int32, lg.shape, 1)
    gmax = jnp.max(lg, axis=-1, keepdims=True)
    g_sel = jnp.min(jnp.where(lg == gmax, gidx, n_groups), axis=-1, keepdims=True)
    g_w = 1.0 / jnp.sum(jnp.exp(lg - gmax), axis=-1, keepdims=True)
    eidx = lax.broadcasted_iota(jnp.int32, el.shape, 1)
    in_group = (eidx >= g_sel * per_group) & (eidx < (g_sel + 1) * per_group)
    e1 = jnp.where(in_group, el, -jnp.inf)
    v1 = jnp.max(e1, axis=-1, keepdims=True)
    i1 = jnp.min(jnp.where(e1 == v1, eidx, n_exp), axis=-1, keepdims=True)
    e2 = jnp.where(eidx == i1, -jnp.inf, e1)
    v2 = jnp.max(e2, axis=-1, keepdims=True)
    i2 = jnp.min(jnp.where(e2 == v2, eidx, n_exp), axis=-1, keepdims=True)
    t = jnp.exp(v2 - v1)
    w1 = g_w / (1.0 + t)
    w2 = g_w * t / (1.0 + t)
    comb_ref[...] = jnp.where(eidx == i1, w1, 0.0) + jnp.where(eidx == i2, w2, 0.0)


def _merge(x, o_gla, o_diff, z, b_ga, b_gb, w_a, w_b, w_o, ln_g, ln_b, w_r, b_r, *, tm, alpha,
           n_groups, per_group, gate_block):
    n, d = x.shape
    n_exp = n_groups * per_group
    row = lambda i: (i, 0)
    fixed = lambda i: (0, 0)
    return pl.pallas_call(
        functools.partial(_merge_kernel, alpha=alpha, n_groups=n_groups, per_group=per_group),
        grid=(n // tm,),
        in_specs=[
            pl.BlockSpec((tm, d), row), pl.BlockSpec((tm, d), row), pl.BlockSpec((tm, d), row),
            pl.BlockSpec((tm, d), lambda i: (i, gate_block)),
            pl.BlockSpec((tm, d), lambda i: (i, gate_block + 1)),
            pl.BlockSpec((1, d), fixed), pl.BlockSpec((1, d), fixed),
            pl.BlockSpec((d, d), fixed), pl.BlockSpec((d, d), fixed), pl.BlockSpec((d, d), fixed),
            pl.BlockSpec((1, d), fixed), pl.BlockSpec((1, d), fixed),
            pl.BlockSpec((d, n_groups + n_exp), fixed), pl.BlockSpec((1, n_groups + n_exp), fixed),
        ],
        out_specs=[pl.BlockSpec((tm, d), row), pl.BlockSpec((tm, n_exp), row)],
        out_shape=[jax.ShapeDtypeStruct((n, d), F32), jax.ShapeDtypeStruct((n, n_exp), F32)],
        compiler_params=_params("parallel"),
        name="merge_ln_route",
    )(x, o_gla, o_diff, z, z, b_ga, b_gb, w_a, w_b, w_o, ln_g, ln_b, w_r, b_r)


def _moe_kernel(h_ref, comb_ref, wg_ref, wu_ref, wd_ref, lg_ref, lb_ref, y_ref, acc_ref, hb_ref, rank_ref,
                rank_t_ref, comb_t_ref, *, alpha, cap):
    e = pl.program_id(1)
    tm, n_exp = comb_ref.shape
    exact = lax.Precision.HIGHEST

    @pl.when(e == 0)
    def _():
        acc_ref[...] = jnp.zeros(acc_ref.shape, F32)
        hb_ref[...] = h_ref[...].astype(BF16)
        comb = comb_ref[...]
        sel = (comb > 0.0).astype(BF16)
        before = (lax.broadcasted_iota(jnp.int32, (tm, tm), 1) < lax.broadcasted_iota(jnp.int32, (tm, tm), 0))
        rank = _dot(before.astype(BF16), sel)
        rank_ref[...] = rank
        eye = (lax.broadcasted_iota(jnp.int32, (n_exp, n_exp), 0)
               == lax.broadcasted_iota(jnp.int32, (n_exp, n_exp), 1)).astype(F32)
        nt = (((1,), (1,)), ((), ()))
        rank_t_ref[...] = lax.dot_general(eye, rank, nt, precision=exact, preferred_element_type=F32)
        comb_t_ref[...] = lax.dot_general(eye, comb, nt, precision=exact, preferred_element_type=F32)

    eidx = lax.broadcasted_iota(jnp.int32, (tm, n_exp), 1)
    w_col = jnp.sum(jnp.where(eidx == e, comb_ref[...], 0.0), axis=-1, keepdims=True)
    rank_col = jnp.sum(jnp.where(eidx == e, rank_ref[...], 0.0), axis=-1, keepdims=True)
    w_row = comb_t_ref[pl.ds(e, 1), :]
    rank_row = rank_t_ref[pl.ds(e, 1), :]
    count = jnp.max(jnp.where(w_col > 0.0, rank_col + 1.0, 0.0)).astype(jnp.int32)
    slot_r = lax.broadcasted_iota(jnp.int32, (cap, tm), 0).astype(F32)
    slot_c = lax.broadcasted_iota(jnp.int32, (tm, cap), 1).astype(F32)

    def block(b, carry):
        base = (b * cap).astype(F32)
        pick = ((rank_row - base) == slot_r) & (w_row > 0.0)
        x = _dot(pick.astype(BF16), hb_ref[...]).astype(BF16)
        gate = _dot(x, wg_ref[0])
        up = _dot(x, wu_ref[0])
        hid = gate * _sigmoid(gate) * up
        w_sel = jnp.sum(jnp.where(pick, w_row, 0.0), axis=-1, keepdims=True)
        out = (w_sel * _dot(hid.astype(BF16), wd_ref[0])).astype(BF16)
        place = ((rank_col - base) == slot_c) & (w_col > 0.0)
        acc_ref[...] += _dot(place.astype(BF16), out)
        return carry

    lax.fori_loop(0, (count + cap - 1) // cap, block, 0)

    @pl.when(e == pl.num_programs(1) - 1)
    def _():
        u = alpha * h_ref[...] + acc_ref[...]
        mu = jnp.mean(u, axis=-1, keepdims=True)
        var = jnp.mean(jnp.square(u - mu), axis=-1, keepdims=True)
        y_ref[...] = (u - mu) * lax.rsqrt(var + LN_EPS) * lg_ref[...] + lb_ref[...]


def _moe(h, comb, w_gate, w_up, w_down, ln_g, ln_b, *, tm, alpha, cap):
    n, d = h.shape
    n_exp, _, d_exp = w_gate.shape
    return pl.pallas_call(
        functools.partial(_moe_kernel, alpha=alpha, cap=cap),
        grid=(n // tm, n_exp),
        in_specs=[
            pl.BlockSpec((tm, d), lambda i, e: (i, 0)),
            pl.BlockSpec((tm, n_exp), lambda i, e: (i, 0)),
            pl.BlockSpec((1, d, d_exp), lambda i, e: (e, 0, 0)),
            pl.BlockSpec((1, d, d_exp), lambda i, e: (e, 0, 0)),
            pl.BlockSpec((1, d_exp, d), lambda i, e: (e, 0, 0)),
            pl.BlockSpec((1, d), lambda i, e: (0, 0)),
            pl.BlockSpec((1, d), lambda i, e: (0, 0)),
        ],
        out_specs=pl.BlockSpec((tm, d), lambda i, e: (i, 0)),
        out_shape=jax.ShapeDtypeStruct((n, d), F32),
        scratch_shapes=[pltpu.VMEM((tm, d), F32), pltpu.VMEM((tm, d), BF16), pltpu.VMEM((tm, n_exp), F32),
                        pltpu.VMEM((n_exp, tm), F32), pltpu.VMEM((n_exp, tm), F32)],
        compiler_params=_params("parallel", "arbitrary"),
        name="moe_ln",
    )(h, comb, w_gate, w_up, w_down, ln_g, ln_b)


def _rotary_tables(pos, rot_dim, reps):
    half = rot_dim // 2
    inv_freq = ROPE_THETA ** (-jnp.arange(0, rot_dim, 2, dtype=F32) / rot_dim)
    ang = pos[:, None] * inv_freq[None, :]
    cos, sin = jnp.cos(ang), jnp.sin(ang)
    t = pos.shape[0]
    zeros = jnp.zeros((t, LANES - rot_dim), F32)
    zh = jnp.zeros((t, half), F32)
    cos_t = jnp.concatenate([cos, cos, jnp.ones((t, LANES - rot_dim), F32)], axis=1)
    sa_t = jnp.concatenate([-sin, zh, zeros], axis=1)
    sb_t = jnp.concatenate([zh, sin, zeros], axis=1)
    return tuple(jnp.tile(a, (reps, 1)) for a in (cos_t, sa_t, sb_t))


def _pick(n, pref):
    t = min(n, pref)
    while n % t:
        t //= 2
    return t


def kernel(x_prompt, x_sample, cache_k, cache_v, state_gla, page_table, w_in, b_gates, w_gla_g2, b_gla_g2,
           gla_norm_g, w_gla_out, lambda_q1, lambda_k1, lambda_q2, lambda_k2, diff_norm_g, w_diff_out, w_o,
           ln1_g, ln1_b, w_group_router, b_group_router, w_expert_router, b_expert_router, w_e_gate, w_e_up,
           w_e_down, ln2_g, ln2_b):
    bsz, seq, d = x_prompt.shape
    dbsz, dseq, _ = x_sample.shape
    depth = w_in.shape[0]
    _, n_pool, page, heads, _, dh = cache_k.shape
    dv = cache_v.shape[-1]
    _, _, gheads, gdk, gdv = state_gla.shape
    rank = w_gla_g2.shape[1]
    n_groups = w_group_router.shape[-1]
    n_exp = w_expert_router.shape[-1]
    per_group = n_exp // n_groups
    rot_dim = dh // 4
    alpha = (2.0 * depth) ** 0.25
    past_len = page_table.shape[1] * page
    hk = gheads * gdk
    assert hk * 2 == d and gheads * gdv == d and heads * 2 * dh == d and heads * dv == d
    o_glr = 2 * hk + d
    o_gr = o_glr + rank
    o_dq = o_gr + d

    np_tok, ns_tok = bsz * seq, dbsz * dseq
    tpad = SUBLANES
    assert dseq <= tpad
    tm_p = _pick(np_tok, 1024)
    tm_s = _pick(ns_tok, 512)
    chunk = _pick(seq, 128)
    tq = _pick(seq, 512)
    wide = _pick(seq, 1024)
    pages = _pick(page_table.shape[1], 16)
    assert seq % tm_p == 0 or tm_p % seq == 0

    pos_p = jnp.arange(seq, dtype=F32)
    pos_s = (past_len + jnp.arange(dseq)).astype(F32)
    tabs_p = _rotary_tables(pos_p, rot_dim, max(1, tm_p // seq))
    tabs_s = _rotary_tables(pos_s, rot_dim, tm_s // dseq)
    q_scale = dh ** -0.5 * LOG2E

    xp = x_prompt.reshape(np_tok, d)
    xs = x_sample.reshape(ns_tok, d)
    outs = [[] for _ in range(6)]
    for l in range(depth):
        lam_init = _lambda_init(l)
        w = w_in[l]
        w_main = jnp.concatenate([w[:, :o_glr], w[:, o_gr:o_dq], w[:, o_dq + 3 * d:], w[:, o_dq:o_dq + 3 * d]],
                                 axis=1).astype(BF16)
        w_glr = w[:, o_glr:o_gr].astype(BF16)
        w_g2 = w_gla_g2[l].astype(BF16)
        b_g2 = b_gla_g2[l].reshape(1, hk)
        b_ga = b_gates[l, :d].reshape(1, d)
        b_gb = b_gates[l, d:].reshape(1, d)
        w_a, w_b, w_out = (t[l].astype(BF16) for t in (w_gla_out, w_diff_out, w_o))
        w_r = jnp.concatenate([w_group_router[l], w_expert_router[l]], axis=1).astype(BF16)
        b_r = jnp.concatenate([b_group_router[l], b_expert_router[l]]).reshape(1, n_groups + n_exp)
        w_eg, w_eu, w_ed = (t[l].astype(BF16) for t in (w_e_gate, w_e_up, w_e_down))
        g_gla = gla_norm_g[l].reshape(1, gdv)
        g_diff = diff_norm_g[l].reshape(1, dv)
        ln1 = (ln1_g[l].reshape(1, d), ln1_b[l].reshape(1, d))
        ln2 = (ln2_g[l].reshape(1, d), ln2_b[l].reshape(1, d))
        lam = (jnp.exp(jnp.sum(lambda_q1[l] * lambda_k1[l])) - jnp.exp(jnp.sum(lambda_q2[l] * lambda_k2[l]))
               + lam_init).astype(F32).reshape(1)

        def head(x, tabs, tm):
            return _in_proj(x, w_main, w_glr, w_g2, b_g2, *tabs, tm=tm, rot_dim=rot_dim, q_scale=q_scale,
                            heads=heads)

        def v_logical(v_stored, lead):
            v5 = v_stored.reshape(*lead, dv // dh, heads, dh)
            return jnp.swapaxes(v5, -3, -2).reshape(*lead, heads, dv)

        def tail(x, o_gla, o_diff, z):
            h, comb = _merge(x, o_gla, o_diff, z, b_ga, b_gb, w_a, w_b, w_out, *ln1, w_r, b_r,
                             tm=_pick(x.shape[0], 512), alpha=alpha, n_groups=n_groups, per_group=per_group,
                             gate_block=3)
            return _moe(h, comb, w_eg, w_eu, w_ed, *ln2, tm=_pick(x.shape[0], 1024), alpha=alpha, cap=160)

        z, la, q_b, k_p, k_b, v_p, v_b = head(xp, tabs_p, tm_p)
        s0 = jnp.zeros((bsz, gheads, gdk, gdv), F32)
        o_gla, s_p = _gla(z.reshape(bsz, seq, -1), la.reshape(bsz, seq, hk), s0, g_gla, chunk=chunk,
                          per_step=2 if seq % (2 * chunk) == 0 else 1, heads=gheads, dk=gdk, dv=gdv)
        o_diff = _flash(lam, q_b.reshape(bsz, seq, d), k_b.reshape(bsz, seq, d), v_b.reshape(bsz, seq, d),
                        g_diff, tq=tq, wide=wide, heads=heads, dh=dh, dv=dv, out_scale=1.0 - lam_init)
        xp = tail(xp, o_gla.reshape(np_tok, d), o_diff.reshape(np_tok, d), z)
        outs[0].append(k_p.reshape(bsz, seq, heads, 2, dh))
        outs[1].append(v_logical(v_p, (bsz, seq)))
        outs[2].append(s_p)

        z, la, q_b, k_s, _, v_s, _ = head(xs, tabs_s, tm_s)
        pad = ((0, 0), (0, tpad - dseq), (0, 0))
        pad3 = lambda a: jnp.pad(a.reshape(dbsz, dseq, -1), pad)
        o_gla, s_s = _gla(pad3(z), pad3(la), state_gla[l], g_gla, chunk=tpad, per_step=1, heads=gheads,
                          dk=gdk, dv=gdv)
        q_rows = jnp.transpose(pad3(q_b).reshape(dbsz, tpad, heads, 2, dh), (0, 2, 3, 1, 4))
        q_rows = q_rows.reshape(dbsz, heads * 2 * tpad, dh)
        per_tok = d // dh
        pad_rows = lambda a: jnp.pad(a.reshape(dbsz, dseq * per_tok, dh),
                                     ((0, 0), (0, (tpad - dseq) * per_tok), (0, 0)))
        o_diff = _decode(page_table, lam, q_rows, pad_rows(k_s), pad_rows(v_s), g_diff, cache_k, cache_v, layer=l,
                         pages=pages, tpad=tpad, n_new=dseq, out_scale=1.0 - lam_init)
        xs = tail(xs, o_gla[:, :dseq].reshape(ns_tok, d), o_diff[:, :dseq].reshape(ns_tok, d), z)
        outs[3].append(k_s.reshape(dbsz, dseq, heads, 2, dh))
        outs[4].append(v_logical(v_s, (dbsz, dseq)))
        outs[5].append(s_s)

    k_prompt, v_prompt, s_prompt, k_sample, v_sample, s_sample = (jnp.stack(o) for o in outs)
    return (xp.reshape(bsz, seq, d), xs.reshape(dbsz, dseq, d), k_prompt, v_prompt, s_prompt,
            k_sample, v_sample, s_sample)
```

```python
import functools
import math

import jax
import jax.numpy as jnp
from jax import lax
from jax.experimental import pallas as pl
from jax.experimental.pallas import tpu as pltpu

F32 = jnp.float32
BF16 = jnp.bfloat16

GLA_TAU = 16.0
ROPE_THETA = 500000.0
LN_EPS = 1e-5
RMS_EPS = 1e-5
LOG2E = 1.4426950408889634
LANES = 128
SUBLANES = 8
VMEM_LIMIT = 56 * 1024 * 1024
NEG_BIG = -1e30


def _lambda_init(layer):
    return 0.8 - 0.6 * math.exp(-0.3 * layer)


def _sigmoid(x):
    return 1.0 / (1.0 + jnp.exp(-x))


def _dot(a, b):
    return jnp.dot(a, b, preferred_element_type=F32)


def _dot_nt(a, b):
    return lax.dot_general(a, b, (((1,), (1,)), ((), ())), preferred_element_type=F32)


def _dot_tn(a, b):
    return lax.dot_general(a, b, (((0,), (0,)), ((), ())), preferred_element_type=F32)


def _params(*sem):
    return pltpu.CompilerParams(dimension_semantics=sem, vmem_limit_bytes=VMEM_LIMIT)


def _in_proj_kernel(x_ref, w_ref, wglr_ref, wg2_ref, bg2_ref, cos_ref, sa_ref, sb_ref,
                    z_ref, la_ref, qb_ref, k_ref, kb_ref, v_ref, vb_ref, *, n_plain, rot_dim, q_scale, heads):
    j = pl.program_id(1)
    xb = x_ref[...].astype(BF16)
    acc = _dot(xb, w_ref[...])

    def rotary(a):
        half = rot_dim // 2
        cos, sa, sb = cos_ref[...], sa_ref[...], sb_ref[...]
        out = []
        for g in range(a.shape[1] // LANES):
            xg = a[:, g * LANES:(g + 1) * LANES]
            up = pltpu.roll(xg, LANES - half, 1)
            dn = pltpu.roll(xg, half, 1)
            out.append(xg * cos + up * sa + dn * sb)
        return jnp.concatenate(out, axis=1)

    @pl.when(j < n_plain)
    def _():
        z_ref[...] = acc

    @pl.when(j == n_plain)
    def _():
        qb_ref[...] = (rotary(acc) * q_scale).astype(BF16)

    tm = acc.shape[0]
    per_tok = acc.shape[1] // LANES

    @pl.when(j == n_plain + 1)
    def _():
        r = rotary(acc)
        kb_ref[...] = r.astype(BF16)
        for g in range(per_tok):
            k_ref[pl.ds(g, tm, stride=per_tok), :] = r[:, g * LANES:(g + 1) * LANES]

    @pl.when(j == n_plain + 2)
    def _():
        vb_ref[...] = acc.astype(BF16)
        chunks = per_tok // heads
        for g in range(per_tok):
            c, h = g // heads, g % heads
            col = (h * chunks + c) * LANES
            v_ref[pl.ds(g, tm, stride=per_tok), :] = acc[:, col:col + LANES]

    @pl.when(j == 0)
    def _():
        glr = _dot(xb, wglr_ref[...])
        logit = _dot(glr.astype(BF16), wg2_ref[...]) + bg2_ref[...]
        ls = jnp.minimum(logit, 0.0) - jnp.log(1.0 + jnp.exp(-jnp.abs(logit)))
        la_ref[...] = ls * (1.0 / GLA_TAU)


def _in_proj(x, w_main, w_glr, w_g2, b_g2, cos_t, sa_t, sb_t, *, tm, rot_dim, q_scale, heads):
    n, d = x.shape
    n_col = w_main.shape[1] // d
    n_plain = n_col - 3
    tab_blocks = cos_t.shape[0] // tm
    rank = w_glr.shape[1]
    dla = w_g2.shape[1]
    per_tok = d // LANES
    tab_spec = pl.BlockSpec((tm, LANES), lambda i, j: (i % tab_blocks, 0))
    row_spec = pl.BlockSpec((tm, d), lambda i, j: (i, 0))
    stored_shape = jax.ShapeDtypeStruct((n * per_tok, LANES), F32)

    def once(shape):
        return pl.BlockSpec(shape, lambda i, j: (i, 0), pipeline_mode=pl.Buffered(1))

    return pl.pallas_call(
        functools.partial(_in_proj_kernel, n_plain=n_plain, rot_dim=rot_dim, q_scale=q_scale, heads=heads),
        grid=(n // tm, n_col),
        in_specs=[
            row_spec,
            pl.BlockSpec((d, d), lambda i, j: (0, j)),
            pl.BlockSpec((d, rank), lambda i, j: (0, 0)),
            pl.BlockSpec((rank, dla), lambda i, j: (0, 0)),
            pl.BlockSpec((1, dla), lambda i, j: (0, 0)),
            tab_spec, tab_spec, tab_spec,
        ],
        out_specs=[
            pl.BlockSpec((tm, d), lambda i, j: (i, jnp.minimum(j, n_plain - 1))),
            once((tm, dla)), once((tm, d)), once((tm * per_tok, LANES)), once((tm, d)),
            once((tm * per_tok, LANES)), once((tm, d)),
        ],
        out_shape=[jax.ShapeDtypeStruct((n, n_plain * d), F32),
                   jax.ShapeDtypeStruct((n, dla), F32),
                   jax.ShapeDtypeStruct((n, d), BF16),
                   stored_shape, jax.ShapeDtypeStruct((n, d), BF16),
                   stored_shape, jax.ShapeDtypeStruct((n, d), BF16)],
        compiler_params=_params("parallel", "arbitrary"),
        name="in_proj",
    )(x, w_main, w_glr, w_g2, b_g2, cos_t, sa_t, sb_t)


def _gla_kernel(q_ref, k_ref, v_ref, la_ref, r_ref, s0_ref, g_ref, o_ref, s_ref, *, heads, dk, dv, chunk):
    c = pl.program_id(1)

    @pl.when(c == 0)
    def _():
        s_ref[...] = s0_ref[...]

    ct = chunk
    row = lax.broadcasted_iota(jnp.int32, (ct, ct), 0)
    col = lax.broadcasted_iota(jnp.int32, (ct, ct), 1)
    causal = col <= row
    tri = causal.astype(F32)
    eye = (lax.broadcasted_iota(jnp.int32, (dk, dk), 0) == lax.broadcasted_iota(jnp.int32, (dk, dk), 1))
    gain = g_ref[...]
    op = (lambda a: a.astype(BF16)) if ct >= 16 else (lambda a: a)

    def one_chunk(rows):
        la = la_ref[0, rows, :]
        b = jnp.dot(tri, la, preferred_element_type=F32, precision=lax.Precision.HIGHEST)
        b_last = b[ct - 1:ct, :]
        kk = k_ref[0, rows, :]
        q_t = q_ref[0, rows, :] * (dk ** -0.5) * jnp.exp(b)
        k_t = kk * jnp.exp(-b)
        k_d = kk * jnp.exp(b_last - b)
        d_row = jnp.exp(b_last)
        for h in range(heads):
            ks = slice(h * dk, (h + 1) * dk)
            vs = slice(h * dv, (h + 1) * dv)
            qh = op(q_t[:, ks])
            vh = op(v_ref[0, rows, vs])
            s_old = s_ref[0, h]
            scores = jnp.where(causal, _dot_nt(qh, op(k_t[:, ks])), 0.0)
            o = _dot(op(scores), vh) + _dot(qh, op(s_old))
            d_col = jnp.sum(jnp.where(eye, jnp.broadcast_to(d_row[:, ks], (dk, dk)), 0.0), axis=1,
                            keepdims=True)
            s_ref[0, h] = d_col * s_old + _dot_tn(op(k_d[:, ks]), vh)
            o = o * lax.rsqrt(jnp.mean(o * o, axis=-1, keepdims=True) + RMS_EPS) * gain
            r = r_ref[0, rows, vs]
            o_ref[0, rows, vs] = o * (r * _sigmoid(r))

    for sub in range(la_ref.shape[1] // chunk):
        one_chunk(slice(sub * chunk, (sub + 1) * chunk))


def _gla(z, la, s0, gain, *, chunk, per_step, heads, dk, dv):
    bsz, t, _ = z.shape
    hk, hv = heads * dk, heads * dv
    blk = chunk * per_step
    return pl.pallas_call(
        functools.partial(_gla_kernel, heads=heads, dk=dk, dv=dv, chunk=chunk),
        grid=(bsz, t // blk),
        in_specs=[
            pl.BlockSpec((1, blk, hk), lambda b, c: (b, c, 0)),
            pl.BlockSpec((1, blk, hk), lambda b, c: (b, c, 1)),
            pl.BlockSpec((1, blk, hv), lambda b, c: (b, c, 1)),
            pl.BlockSpec((1, blk, hk), lambda b, c: (b, c, 0)),
            pl.BlockSpec((1, blk, hv), lambda b, c: (b, c, 2)),
            pl.BlockSpec((1, heads, dk, dv), lambda b, c: (b, 0, 0, 0)),
            pl.BlockSpec((1, dv), lambda b, c: (0, 0)),
        ],
        out_specs=[
            pl.BlockSpec((1, blk, hv), lambda b, c: (b, c, 0)),
            pl.BlockSpec((1, heads, dk, dv), lambda b, c: (b, 0, 0, 0)),
        ],
        out_shape=[jax.ShapeDtypeStruct((bsz, t, hv), F32),
                   jax.ShapeDtypeStruct((bsz, heads, dk, dv), F32)],
        compiler_params=_params("parallel", "arbitrary"),
        name="gla",
    )(z, z, z, la, z, s0, gain)


def _softmax_update(s, m_ref, l_ref, acc_ref, pv_fn):
    m_old = m_ref[...]
    m_new = jnp.maximum(m_old, jnp.max(s, axis=-1, keepdims=True))
    alpha = jnp.exp2(m_old - m_new)
    width = s.shape[1]
    if width % LANES == 0:
        ps = [jnp.exp2(s[:, k * LANES:(k + 1) * LANES] - m_new) for k in range(width // LANES)]
        psum = functools.reduce(jnp.add, ps)
        p = jnp.concatenate(ps, axis=1) if len(ps) > 1 else ps[0]
    else:
        p = jnp.exp2(s - m_new[:, 0:1])
        lane = lax.broadcasted_iota(jnp.int32, m_old.shape, 1)
        psum = jnp.where(lane == 0, jnp.sum(p, axis=-1, keepdims=True), 0.0)
    l_ref[...] = alpha * l_ref[...] + psum
    reps = acc_ref.shape[1] // LANES
    alpha_w = jnp.concatenate([alpha] * reps, axis=1) if reps > 1 else alpha
    acc_ref[...] = alpha_w * acc_ref[...] + pv_fn(p)
    m_ref[...] = m_new


def _softmax_init(m_ref, l_ref, acc_ref):
    m_ref[...] = jnp.full(m_ref.shape, NEG_BIG, F32)
    l_ref[...] = jnp.zeros(l_ref.shape, F32)
    acc_ref[...] = jnp.zeros(acc_ref.shape, F32)


def _flash_kernel(lam_ref, q_ref, k_ref, v_ref, g_ref, o_ref, m_ref, l_ref, acc_ref, *, tq, wide, dh,
                  out_scale):
    qi = pl.program_id(2)
    _softmax_init(m_ref, l_ref, acc_ref)
    q1 = q_ref[0, :, 0:dh]
    q2 = q_ref[0, :, dh:2 * dh]

    def step(start, width, masked):
        vb = v_ref[0, pl.ds(start, width), :]
        s1 = _dot_nt(q1, k_ref[0, pl.ds(start, width), 0:dh])
        s2 = _dot_nt(q2, k_ref[0, pl.ds(start, width), dh:2 * dh])
        s = jnp.concatenate([s1, s2], axis=0)
        if masked:
            row = lax.broadcasted_iota(jnp.int32, s.shape, 0) % tq
            col = lax.broadcasted_iota(jnp.int32, s.shape, 1)
            s = jnp.where(col <= row, s, NEG_BIG)
        _softmax_update(s, m_ref, l_ref, acc_ref, lambda p: _dot(p.astype(BF16), vb))

    q_start = qi * tq
    n_wide = q_start // wide

    def wide_body(j, carry):
        step(pl.multiple_of(j * wide, wide), wide, False)
        return carry

    lax.fori_loop(0, n_wide, wide_body, 0)

    def narrow_body(j, carry):
        step(pl.multiple_of(n_wide * wide + j * tq, tq), tq, False)
        return carry

    lax.fori_loop(0, (q_start - n_wide * wide) // tq, narrow_body, 0)
    step(pl.multiple_of(q_start, tq), tq, True)

    lam = lam_ref[0]
    l = jnp.sum(l_ref[...], axis=-1, keepdims=True)
    o = acc_ref[0:tq, :] / l[0:tq] - lam * (acc_ref[tq:, :] / l[tq:])
    o = o * lax.rsqrt(jnp.mean(o * o, axis=-1, keepdims=True) + RMS_EPS) * g_ref[...]
    o_ref[0] = o * out_scale


def _flash(lam, q, k, v, gain, *, tq, wide, heads, dh, dv, out_scale):
    bsz, t, _ = q.shape
    return pl.pallas_call(
        functools.partial(_flash_kernel, tq=tq, wide=wide, dh=dh, out_scale=out_scale),
        grid=(bsz, heads, t // tq),
        in_specs=[
            pl.BlockSpec(memory_space=pltpu.SMEM),
            pl.BlockSpec((1, tq, 2 * dh), lambda b, h, i: (b, i, h)),
            pl.BlockSpec((1, t, 2 * dh), lambda b, h, i: (b, 0, h)),
            pl.BlockSpec((1, t, dv), lambda b, h, i: (b, 0, h)),
            pl.BlockSpec((1, dv), lambda b, h, i: (0, 0)),
        ],
        out_specs=pl.BlockSpec((1, tq, dv), lambda b, h, i: (b, i, h)),
        out_shape=jax.ShapeDtypeStruct((bsz, t, heads * dv), F32),
        scratch_shapes=[pltpu.VMEM((2 * tq, LANES), F32), pltpu.VMEM((2 * tq, LANES), F32),
                        pltpu.VMEM((2 * tq, dv), F32)],
        compiler_params=_params("parallel", "parallel", "arbitrary"),
        name="flash_diff_attn",
    )(lam, q, k, v, gain)


def _decode_kernel(pt_ref, lam_ref, q_ref, kn_ref, vn_ref, g_ref, *rest, pages, page, heads, dh, dv, tpad,
                   n_new, out_scale):
    k_refs = rest[:pages]
    v_refs = rest[pages:2 * pages]
    o_ref = rest[2 * pages]
    m_ref, l_ref, acc_ref = rest[2 * pages + 1:]
    c = pl.program_id(1)
    hrows = 2 * tpad

    @pl.when(c == 0)
    def _():
        _softmax_init(m_ref, l_ref, acc_ref)

    q = q_ref[0]
    first = lax.broadcasted_iota(jnp.int32, (hrows, page), 0) < tpad

    def head_scores(h, k1, k2):
        qh = q[h * hrows:(h + 1) * hrows, :]
        return jnp.where(first[:, :k1.shape[0]], _dot_nt(qh, k1), _dot_nt(qh, k2))

    per_tok = heads * 2

    def k_rows(i, h, comp):
        return k_refs[i][0, 0, pl.ds(2 * h + comp, page, stride=per_tok), :].astype(BF16)

    def v_rows(i, h):
        return jnp.concatenate([v_refs[i][0, 0, pl.ds(heads * j + h, page, stride=per_tok), :]
                                for j in range(dv // dh)], axis=1).astype(BF16)

    s = jnp.concatenate(
        [jnp.concatenate([head_scores(h, k_rows(i, h, 0), k_rows(i, h, 1)) for i in range(pages)], axis=1)
         for h in range(heads)], axis=0)

    def pv_pages(p):
        p = p.astype(BF16)
        outs = []
        for h in range(heads):
            ph = p[h * hrows:(h + 1) * hrows, :]
            o = _dot(ph[:, 0:page], v_rows(0, h))
            for i in range(1, pages):
                o = o + _dot(ph[:, i * page:(i + 1) * page], v_rows(i, h))
            outs.append(o)
        return jnp.concatenate(outs, axis=0)

    _softmax_update(s, m_ref, l_ref, acc_ref, pv_pages)

    @pl.when(c == pl.num_programs(1) - 1)
    def _():
        qf = q.astype(F32)

        def kn_rows(h, comp):
            return kn_ref[0, pl.ds(2 * h + comp, tpad, stride=per_tok), :]

        def vn_rows(h):
            return jnp.concatenate([vn_ref[0, pl.ds(heads * j + h, tpad, stride=per_tok), :]
                                    for j in range(dv // dh)], axis=1)

        sn = jnp.concatenate(
            [jnp.where(first[:, :tpad],
                       _dot_nt(qf[h * hrows:(h + 1) * hrows, :], kn_rows(h, 0)),
                       _dot_nt(qf[h * hrows:(h + 1) * hrows, :], kn_rows(h, 1)))
             for h in range(heads)], axis=0)
        tok = lax.broadcasted_iota(jnp.int32, sn.shape, 0) % tpad
        key = lax.broadcasted_iota(jnp.int32, sn.shape, 1)
        sn = jnp.where((key <= tok) & (key < n_new), sn, NEG_BIG)

        def pv_new(p):
            return jnp.concatenate([_dot(p[h * hrows:(h + 1) * hrows, :], vn_rows(h))
                                    for h in range(heads)], axis=0)

        _softmax_update(sn, m_ref, l_ref, acc_ref, pv_new)
        lam = lam_ref[0]
        l = jnp.sum(l_ref[...], axis=-1, keepdims=True)
        a = acc_ref[...] / l
        for h in range(heads):
            oh = a[h * hrows:h * hrows + tpad, :] - lam * a[h * hrows + tpad:(h + 1) * hrows, :]
            oh = oh * lax.rsqrt(jnp.mean(oh * oh, axis=-1, keepdims=True) + RMS_EPS) * g_ref[...]
            o_ref[0, :, h * dv:(h + 1) * dv] = oh * out_scale


def _decode(page_table, lam, q_rows, k_new, v_new, gain, cache_k, cache_v, *, layer, pages, tpad, n_new,
            out_scale):
    dbsz, rows, dh = q_rows.shape
    depth, n_pool, page, heads, dv = cache_v.shape
    d = heads * dv
    n_chunks = page_table.shape[1] // pages
    per_page = page * heads * 2
    k_view = cache_k.reshape(depth, n_pool, per_page, dh)
    v_view = jnp.transpose(cache_v.reshape(depth, n_pool, page, heads, dv // dh, dh), (0, 1, 2, 4, 3, 5))
    v_view = v_view.reshape(depth, n_pool, per_page, dh)

    def page_spec(i):
        return pl.BlockSpec((1, 1, per_page, dh), lambda b, c, pt: (layer, pt[b, c * pages + i], 0, 0))

    return pl.pallas_call(
        functools.partial(_decode_kernel, pages=pages, page=page, heads=heads, dh=dh, dv=dv, tpad=tpad,
                          n_new=n_new, out_scale=out_scale),
        grid_spec=pltpu.PrefetchScalarGridSpec(
            num_scalar_prefetch=1,
            grid=(dbsz, n_chunks),
            in_specs=[
                pl.BlockSpec(memory_space=pltpu.SMEM),
                pl.BlockSpec((1, rows, dh), lambda b, c, pt: (b, 0, 0)),
                pl.BlockSpec((1, tpad * heads * 2, dh), lambda b, c, pt: (b, 0, 0)),
                pl.BlockSpec((1, tpad * heads * 2, dh), lambda b, c, pt: (b, 0, 0)),
                pl.BlockSpec((1, dv), lambda b, c, pt: (0, 0)),
            ] + [page_spec(i) for i in range(pages)] + [page_spec(i) for i in range(pages)],
            out_specs=pl.BlockSpec((1, tpad, d), lambda b, c, pt: (b, 0, 0)),
            scratch_shapes=[pltpu.VMEM((rows, LANES), F32), pltpu.VMEM((rows, LANES), F32),
                            pltpu.VMEM((rows, dv), F32)],
        ),
        out_shape=jax.ShapeDtypeStruct((dbsz, tpad, d), F32),
        compiler_params=_params("parallel", "arbitrary"),
        name="paged_diff_attn",
    )(page_table, lam, q_rows, k_new, v_new, gain, *([k_view] * pages), *([v_view] * pages))


def _merge_kernel(x_ref, oa_ref, ob_ref, ga_ref, gb_ref, bga_ref, bgb_ref, wa_ref, wb_ref, wo_ref,
                  lg_ref, lb_ref, wr_ref, br_ref, h_ref, comb_ref, *, alpha, n_groups, per_group, parts):
    size = x_ref.shape[0] // parts
    for part in range(parts):
        _merge_rows(slice(part * size, (part + 1) * size), x_ref, oa_ref, ob_ref, ga_ref, gb_ref, bga_ref,
                    bgb_ref, wa_ref, wb_ref, wo_ref, lg_ref, lb_ref, wr_ref, br_ref, h_ref, comb_ref,
                    alpha=alpha, n_groups=n_groups, per_group=per_group)


def _merge_rows(rows, x_ref, oa_ref, ob_ref, ga_ref, gb_ref, bga_ref, bgb_ref, wa_ref, wb_ref, wo_ref,
                lg_ref, lb_ref, wr_ref, br_ref, h_ref, comb_ref, *, alpha, n_groups, per_group):
    a = _dot(oa_ref[rows, :].astype(BF16), wa_ref[...])
    b = _dot(ob_ref[rows, :].astype(BF16), wb_ref[...])
    g_a = _sigmoid(ga_ref[rows, :] + bga_ref[...])
    g_b = _sigmoid(gb_ref[rows, :] + bgb_ref[...])
    merged = g_a * a + g_b * b
    u = alpha * x_ref[rows, :] + _dot(merged.astype(BF16), wo_ref[...])
    mu = jnp.mean(u, axis=-1, keepdims=True)
    var = jnp.mean(jnp.square(u - mu), axis=-1, keepdims=True)
    h = (u - mu) * lax.rsqrt(var + LN_EPS) * lg_ref[...] + lb_ref[...]
    h_ref[rows, :] = h

    n_exp = n_groups * per_group
    logits = _dot(h.astype(BF16), wr_ref[...]) + br_ref[...]
    lg = logits[:, 0:n_groups]
    el = logits[:, n_groups:n_groups + n_exp]
    gidx = lax.broadcasted_iota(jnp.int32, lg.shape, 1)
    gmax = jnp.max(lg, axis=-1, keepdims=True)
    g_sel = jnp.min(jnp.where(lg == gmax, gidx, n_groups), axis=-1, keepdims=True)
    g_w = 1.0 / jnp.sum(jnp.exp(lg - gmax), axis=-1, keepdims=True)
    eidx = lax.broadcasted_iota(jnp.int32, el.shape, 1)
    in_group = (eidx >= g_sel * per_group) & (eidx < (g_sel + 1) * per_group)
    e1 = jnp.where(in_group, el, -jnp.inf)
    v1 = jnp.max(e1, axis=-1, keepdims=True)
    i1 = jnp.min(jnp.where(e1 == v1, eidx, n_exp), axis=-1, keepdims=True)
    e2 = jnp.where(eidx == i1, -jnp.inf, e1)
    v2 = jnp.max(e2, axis=-1, keepdims=True)
    i2 = jnp.min(jnp.where(e2 == v2, eidx, n_exp), axis=-1, keepdims=True)
    t = jnp.exp(v2 - v1)
    w1 = g_w / (1.0 + t)
    w2 = g_w * t / (1.0 + t)
    comb_ref[rows, :] = jnp.where(eidx == i1, w1, 0.0) + jnp.where(eidx == i2, w2, 0.0)


def _merge(x, o_gla, o_diff, z, b_ga, b_gb, w_a, w_b, w_o, ln_g, ln_b, w_r, b_r, *, tm, alpha,
           n_groups, per_group, gate_block):
    n, d = x.shape
    n_exp = n_groups * per_group
    row = lambda i: (i, 0)
    fixed = lambda i: (0, 0)
    return pl.pallas_call(
        functools.partial(_merge_kernel, alpha=alpha, n_groups=n_groups, per_group=per_group,
                          parts=2 if tm % (2 * SUBLANES * 2) == 0 else 1),
        grid=(n // tm,),
        in_specs=[
            pl.BlockSpec((tm, d), row), pl.BlockSpec((tm, d), row), pl.BlockSpec((tm, d), row),
            pl.BlockSpec((tm, d), lambda i: (i, gate_block)),
            pl.BlockSpec((tm, d), lambda i: (i, gate_block + 1)),
            pl.BlockSpec((1, d), fixed), pl.BlockSpec((1, d), fixed),
            pl.BlockSpec((d, d), fixed), pl.BlockSpec((d, d), fixed), pl.BlockSpec((d, d), fixed),
            pl.BlockSpec((1, d), fixed), pl.BlockSpec((1, d), fixed),
            pl.BlockSpec((d, n_groups + n_exp), fixed), pl.BlockSpec((1, n_groups + n_exp), fixed),
        ],
        out_specs=[pl.BlockSpec((tm, d), row), pl.BlockSpec((tm, n_exp), row)],
        out_shape=[jax.ShapeDtypeStruct((n, d), F32), jax.ShapeDtypeStruct((n, n_exp), F32)],
        compiler_params=_params("parallel"),
        name="merge_ln_route",
    )(x, o_gla, o_diff, z, z, b_ga, b_gb, w_a, w_b, w_o, ln_g, ln_b, w_r, b_r)


def _moe_kernel(h_ref, comb_ref, wg_ref, wu_ref, wd_ref, lg_ref, lb_ref, y_ref, acc_ref, hb_ref, rank_ref,
                rank_t_ref, comb_t_ref, *, alpha, cap):
    e = pl.program_id(1)
    tm, n_exp = comb_ref.shape
    exact = lax.Precision.HIGHEST

    @pl.when(e == 0)
    def _():
        acc_ref[...] = jnp.zeros(acc_ref.shape, F32)
        hb_ref[...] = h_ref[...].astype(BF16)
        comb = comb_ref[...]
        sel = (comb > 0.0).astype(BF16)
        before = (lax.broadcasted_iota(jnp.int32, (tm, tm), 1) < lax.broadcasted_iota(jnp.int32, (tm, tm), 0))
        rank = _dot(before.astype(BF16), sel)
        rank_ref[...] = rank
        eye = (lax.broadcasted_iota(jnp.int32, (n_exp, n_exp), 0)
               == lax.broadcasted_iota(jnp.int32, (n_exp, n_exp), 1)).astype(F32)
        nt = (((1,), (1,)), ((), ()))
        rank_t_ref[...] = lax.dot_general(eye, rank, nt, precision=exact, preferred_element_type=F32)
        comb_t_ref[...] = lax.dot_general(eye, comb, nt, precision=exact, preferred_element_type=F32)

    eidx = lax.broadcasted_iota(jnp.int32, (tm, n_exp), 1)
    w_col = jnp.sum(jnp.where(eidx == e, comb_ref[...], 0.0), axis=-1, keepdims=True)
    rank_col = jnp.sum(jnp.where(eidx == e, rank_ref[...], 0.0), axis=-1, keepdims=True)
    w_row = comb_t_ref[pl.ds(e, 1), :]
    rank_row = rank_t_ref[pl.ds(e, 1), :]
    count = jnp.max(jnp.where(w_col > 0.0, rank_col + 1.0, 0.0)).astype(jnp.int32)
    slot_r = lax.broadcasted_iota(jnp.int32, (cap, tm), 0).astype(F32)
    slot_c = lax.broadcasted_iota(jnp.int32, (tm, cap), 1).astype(F32)

    def block(b, carry):
        base = (b * cap).astype(F32)
        pick = ((rank_row - base) == slot_r) & (w_row > 0.0)
        x = _dot(pick.astype(BF16), hb_ref[...]).astype(BF16)
        gate = _dot(x, wg_ref[0])
        up = _dot(x, wu_ref[0])
        hid = gate * _sigmoid(gate) * up
        w_sel = jnp.sum(jnp.where(pick, w_row, 0.0), axis=-1, keepdims=True)
        out = (w_sel * _dot(hid.astype(BF16), wd_ref[0])).astype(BF16)
        place = ((rank_col - base) == slot_c) & (w_col > 0.0)
        acc_ref[...] += _dot(place.astype(BF16), out)
        return carry

    lax.fori_loop(0, (count + cap - 1) // cap, block, 0)

    @pl.when(e == pl.num_programs(1) - 1)
    def _():
        u = alpha * h_ref[...] + acc_ref[...]
        mu = jnp.mean(u, axis=-1, keepdims=True)
        var = jnp.mean(jnp.square(u - mu), axis=-1, keepdims=True)
        y_ref[...] = (u - mu) * lax.rsqrt(var + LN_EPS) * lg_ref[...] + lb_ref[...]


def _moe(h, comb, w_gate, w_up, w_down, ln_g, ln_b, *, tm, alpha, cap):
    n, d = h.shape
    n_exp, _, d_exp = w_gate.shape
    return pl.pallas_call(
        functools.partial(_moe_kernel, alpha=alpha, cap=cap),
        grid=(n // tm, n_exp),
        in_specs=[
            pl.BlockSpec((tm, d), lambda i, e: (i, 0)),
            pl.BlockSpec((tm, n_exp), lambda i, e: (i, 0)),
            pl.BlockSpec((1, d, d_exp), lambda i, e: (e, 0, 0)),
            pl.BlockSpec((1, d, d_exp), lambda i, e: (e, 0, 0)),
            pl.BlockSpec((1, d_exp, d), lambda i, e: (e, 0, 0)),
            pl.BlockSpec((1, d), lambda i, e: (0, 0)),
            pl.BlockSpec((1, d), lambda i, e: (0, 0)),
        ],
        out_specs=pl.BlockSpec((tm, d), lambda i, e: (i, 0)),
        out_shape=jax.ShapeDtypeStruct((n, d), F32),
        scratch_shapes=[pltpu.VMEM((tm, d), F32), pltpu.VMEM((tm, d), BF16), pltpu.VMEM((tm, n_exp), F32),
                        pltpu.VMEM((n_exp, tm), F32), pltpu.VMEM((n_exp, tm), F32)],
        compiler_params=_params("parallel", "arbitrary"),
        name="moe_ln",
    )(h, comb, w_gate, w_up, w_down, ln_g, ln_b)


def _rotary_tables(pos, rot_dim, reps):
    half = rot_dim // 2
    inv_freq = ROPE_THETA ** (-jnp.arange(0, rot_dim, 2, dtype=F32) / rot_dim)
    ang = pos[:, None] * inv_freq[None, :]
    cos, sin = jnp.cos(ang), jnp.sin(ang)
    t = pos.shape[0]
    zeros = jnp.zeros((t, LANES - rot_dim), F32)
    zh = jnp.zeros((t, half), F32)
    cos_t = jnp.concatenate([cos, cos, jnp.ones((t, LANES - rot_dim), F32)], axis=1)
    sa_t = jnp.concatenate([-sin, zh, zeros], axis=1)
    sb_t = jnp.concatenate([zh, sin, zeros], axis=1)
    return tuple(jnp.tile(a, (reps, 1)) for a in (cos_t, sa_t, sb_t))


def _pick(n, pref):
    t = min(n, pref)
    while n % t:
        t //= 2
    return t


def kernel(x_prompt, x_sample, cache_k, cache_v, state_gla, page_table, w_in, b_gates, w_gla_g2, b_gla_g2,
           gla_norm_g, w_gla_out, lambda_q1, lambda_k1, lambda_q2, lambda_k2, diff_norm_g, w_diff_out, w_o,
           ln1_g, ln1_b, w_group_router, b_group_router, w_expert_router, b_expert_router, w_e_gate, w_e_up,
           w_e_down, ln2_g, ln2_b):
    bsz, seq, d = x_prompt.shape
    dbsz, dseq, _ = x_sample.shape
    depth = w_in.shape[0]
    _, n_pool, page, heads, _, dh = cache_k.shape
    dv = cache_v.shape[-1]
    _, _, gheads, gdk, gdv = state_gla.shape
    rank = w_gla_g2.shape[1]
    n_groups = w_group_router.shape[-1]
    n_exp = w_expert_router.shape[-1]
    per_group = n_exp // n_groups
    rot_dim = dh // 4
    alpha = (2.0 * depth) ** 0.25
    past_len = page_table.shape[1] * page
    hk = gheads * gdk
    assert hk * 2 == d and gheads * gdv == d and heads * 2 * dh == d and heads * dv == d
    o_glr = 2 * hk + d
    o_gr = o_glr + rank
    o_dq = o_gr + d

    np_tok, ns_tok = bsz * seq, dbsz * dseq
    tpad = SUBLANES
    assert dseq <= tpad
    tm_p = _pick(np_tok, 1024)
    tm_s = _pick(ns_tok, 512)
    chunk = _pick(seq, 128)
    tq = _pick(seq, 512)
    wide = _pick(seq, 1024)
    pages = _pick(page_table.shape[1], 16)
    assert seq % tm_p == 0 or tm_p % seq == 0

    pos_p = jnp.arange(seq, dtype=F32)
    pos_s = (past_len + jnp.arange(dseq)).astype(F32)
    tabs_p = _rotary_tables(pos_p, rot_dim, max(1, tm_p // seq))
    tabs_s = _rotary_tables(pos_s, rot_dim, tm_s // dseq)
    q_scale = dh ** -0.5 * LOG2E

    xp = x_prompt.reshape(np_tok, d)
    xs = x_sample.reshape(ns_tok, d)
    outs = [[] for _ in range(6)]
    for l in range(depth):
        lam_init = _lambda_init(l)
        w = w_in[l]
        w_main = jnp.concatenate([w[:, :o_glr], w[:, o_gr:o_dq], w[:, o_dq + 3 * d:], w[:, o_dq:o_dq + 3 * d]],
                                 axis=1).astype(BF16)
        w_glr = w[:, o_glr:o_gr].astype(BF16)
        w_g2 = w_gla_g2[l].astype(BF16)
        b_g2 = b_gla_g2[l].reshape(1, hk)
        b_ga = b_gates[l, :d].reshape(1, d)
        b_gb = b_gates[l, d:].reshape(1, d)
        w_a, w_b, w_out = (t[l].astype(BF16) for t in (w_gla_out, w_diff_out, w_o))
        w_r = jnp.concatenate([w_group_router[l], w_expert_router[l]], axis=1).astype(BF16)
        b_r = jnp.concatenate([b_group_router[l], b_expert_router[l]]).reshape(1, n_groups + n_exp)
        w_eg, w_eu, w_ed = (t[l].astype(BF16) for t in (w_e_gate, w_e_up, w_e_down))
        g_gla = gla_norm_g[l].reshape(1, gdv)
        g_diff = diff_norm_g[l].reshape(1, dv)
        ln1 = (ln1_g[l].reshape(1, d), ln1_b[l].reshape(1, d))
        ln2 = (ln2_g[l].reshape(1, d), ln2_b[l].reshape(1, d))
        lam = (jnp.exp(jnp.sum(lambda_q1[l] * lambda_k1[l])) - jnp.exp(jnp.sum(lambda_q2[l] * lambda_k2[l]))
               + lam_init).astype(F32).reshape(1)

        def head(x, tabs, tm):
            return _in_proj(x, w_main, w_glr, w_g2, b_g2, *tabs, tm=tm, rot_dim=rot_dim, q_scale=q_scale,
                            heads=heads)

        def v_logical(v_stored, lead):
            v5 = v_stored.reshape(*lead, dv // dh, heads, dh)
            return jnp.swapaxes(v5, -3, -2).reshape(*lead, heads, dv)

        def tail(x, o_gla, o_diff, z):
            h, comb = _merge(x, o_gla, o_diff, z, b_ga, b_gb, w_a, w_b, w_out, *ln1, w_r, b_r,
                             tm=_pick(x.shape[0], 512), alpha=alpha, n_groups=n_groups, per_group=per_group,
                             gate_block=3)
            return _moe(h, comb, w_eg, w_eu, w_ed, *ln2, tm=_pick(x.shape[0], 1024), alpha=alpha, cap=160)

        z, la, q_b, k_p, k_b, v_p, v_b = head(xp, tabs_p, tm_p)
        s0 = jnp.zeros((bsz, gheads, gdk, gdv), F32)
        o_gla, s_p = _gla(z.reshape(bsz, seq, -1), la.reshape(bsz, seq, hk), s0, g_gla, chunk=chunk,
                          per_step=4 if seq % (4 * chunk) == 0 else 1, heads=gheads, dk=gdk, dv=gdv)
        o_diff = _flash(lam, q_b.reshape(bsz, seq, d), k_b.reshape(bsz, seq, d), v_b.reshape(bsz, seq, d),
                        g_diff, tq=tq, wide=wide, heads=heads, dh=dh, dv=dv, out_scale=1.0 - lam_init)
        xp = tail(xp, o_gla.reshape(np_tok, d), o_diff.reshape(np_tok, d), z)
        outs[0].append(k_p.reshape(bsz, seq, heads, 2, dh))
        outs[1].append(v_logical(v_p, (bsz, seq)))
        outs[2].append(s_p)

        z, la, q_b, k_s, _, v_s, _ = head(xs, tabs_s, tm_s)
        pad = ((0, 0), (0, tpad - dseq), (0, 0))
        pad3 = lambda a: jnp.pad(a.reshape(dbsz, dseq, -1), pad)
        o_gla, s_s = _gla(pad3(z), pad3(la), state_gla[l], g_gla, chunk=tpad, per_step=1, heads=gheads,
                          dk=gdk, dv=gdv)
        q_rows = jnp.transpose(pad3(q_b).reshape(dbsz, tpad, heads, 2, dh), (0, 2, 3, 1, 4))
        q_rows = q_rows.reshape(dbsz, heads * 2 * tpad, dh)
        per_tok = d // dh
        pad_rows = lambda a: jnp.pad(a.reshape(dbsz, dseq * per_tok, dh),
                                     ((0, 0), (0, (tpad - dseq) * per_tok), (0, 0)))
        o_diff = _decode(page_table, lam, q_rows, pad_rows(k_s), pad_rows(v_s), g_diff, cache_k, cache_v, layer=l,
                         pages=pages, tpad=tpad, n_new=dseq, out_scale=1.0 - lam_init)
        xs = tail(xs, o_gla[:, :dseq].reshape(ns_tok, d), o_diff[:, :dseq].reshape(ns_tok, d), z)
        outs[3].append(k_s.reshape(dbsz, dseq, heads, 2, dh))
        outs[4].append(v_logical(v_s, (dbsz, dseq)))
        outs[5].append(s_s)

    k_prompt, v_prompt, s_prompt, k_sample, v_sample, s_sample = (jnp.stack(o) for o in outs)
    return (xp.reshape(bsz, seq, d), xs.reshape(dbsz, dseq, d), k_prompt, v_prompt, s_prompt,
            k_sample, v_sample, s_sample)
```
